```python
import jax
import jax.numpy as jnp
from jax import lax
import numpy as np

D_MODEL = 4096
BATCH = 4
SEQ = 2048
DEPTH = 2
DEC_BATCH = 128
DEC_SEQ = 4
PAST_LEN = 16384
PAGE_SIZE = 128

N_MIXERS = 2
N_MOBA_LAYERS = (DEPTH + 1) // 2
N_MLA_LAYERS = DEPTH // 2
HEAD_DIM = 128
MOBA_Q_HEADS = D_MODEL // HEAD_DIM
MOBA_KV_HEADS = 2
MOBA_GROUP = MOBA_Q_HEADS // MOBA_KV_HEADS
MOBA_BLOCK = 256
MOBA_TOPK = 3
MOBA_Q_CHUNK = 16
MLA_HEADS = D_MODEL // 128
MLA_Q_LORA = 1024
MLA_KV_LORA = 512
MLA_NOPE = 128
MLA_ROPE = 64
MLA_QK_DIM = MLA_NOPE + MLA_ROPE
MLA_V_DIM = 128
MLA_Q_BLOCK = 128
ROPE_THETA = 10000.0
D_FF = 256 * ((8 * D_MODEL // 3 + 255) // 256)
CONV_W = 3
NORM_EPS = 1e-6

kernel_name = 'hybrid_moba_mla_convffn_step'


def rmsnorm(x, g):
    xf = x.astype(jnp.float32)
    y = xf * lax.rsqrt(jnp.mean(xf * xf, axis=-1, keepdims=True) + NORM_EPS)
    return (y * g.astype(jnp.float32)).astype(x.dtype)


def alibi_slopes(n):
    return jnp.exp2(-8.0 * jnp.arange(1, n + 1, dtype=jnp.float32) / n)


def rope(x, pos):
    half = x.shape[-1] // 2
    inv = ROPE_THETA ** (-jnp.arange(half, dtype=jnp.float32) / half)
    ang = pos.astype(jnp.float32)[:, None] * inv[None, :]
    if x.ndim == 4:
        ang = ang[:, None, :]
    cos, sin = jnp.cos(ang), jnp.sin(ang)
    xf = x.astype(jnp.float32)
    x1, x2 = xf[..., :half], xf[..., half:]
    return jnp.concatenate([x1 * cos - x2 * sin, x1 * sin + x2 * cos], axis=-1).astype(x.dtype)


def moba_project(h, w_qkv, g_q, g_k):
    b, t, _ = h.shape
    nq = MOBA_Q_HEADS * HEAD_DIM
    nk = MOBA_KV_HEADS * HEAD_DIM
    qkv = h @ w_qkv
    q = rmsnorm(qkv[..., :nq].reshape(b, t, MOBA_KV_HEADS, MOBA_GROUP, HEAD_DIM), g_q)
    k = rmsnorm(qkv[..., nq:nq + nk].reshape(b, t, MOBA_KV_HEADS, HEAD_DIM), g_k)
    v = qkv[..., nq + nk:].reshape(b, t, MOBA_KV_HEADS, HEAD_DIM)
    return q, k, v


def moba_blocks(k, v):
    L = k.shape[0]
    nb = -(-L // MOBA_BLOCK)
    pad = nb * MOBA_BLOCK - L
    kb = jnp.pad(k, ((0, pad), (0, 0), (0, 0))).reshape(nb, MOBA_BLOCK, MOBA_KV_HEADS, HEAD_DIM).transpose(2, 0, 1, 3)
    vb = jnp.pad(v, ((0, pad), (0, 0), (0, 0))).reshape(nb, MOBA_BLOCK, MOBA_KV_HEADS, HEAD_DIM).transpose(2, 0, 1, 3)
    kmean = jnp.mean(kb.astype(jnp.float32), axis=2)
    return kb, vb, kmean


def moba_attend(q, pos, kb, vb, kmean):
    t = q.shape[0]
    nb = kb.shape[1]
    n_sel = min(MOBA_TOPK, nb)
    cur = pos // MOBA_BLOCK
    qf = q.astype(jnp.float32)
    gate = jnp.einsum('tkgd,knd->tkgn', qf, kmean)
    fully_past = jnp.arange(nb)[None, None, None, :] < cur[:, None, None, None]
    gate = jnp.where(fully_past, gate, -jnp.inf)
    _, sel = lax.top_k(gate, n_sel)
    valid = sel < cur[:, None, None, None]
    own = jnp.broadcast_to(cur[:, None, None, None], sel.shape[:-1] + (1,))
    blk = jnp.concatenate([sel, own], axis=-1)
    kv_idx = jnp.arange(MOBA_KV_HEADS)[None, :, None, None]
    kg = kb[kv_idx, blk]
    vg = vb[kv_idx, blk]
    s = jnp.einsum('tkgd,tkgjsd->tkgjs', qf, kg.astype(jnp.float32)) * (HEAD_DIM ** -0.5)
    key_pos = blk[..., None] * MOBA_BLOCK + jnp.arange(MOBA_BLOCK)
    dist = pos[:, None, None, None, None] - key_pos
    slopes = alibi_slopes(MOBA_Q_HEADS).reshape(MOBA_KV_HEADS, MOBA_GROUP)
    s = s - slopes[None, :, :, None, None] * dist.astype(jnp.float32)
    slot_ok = jnp.concatenate([valid, jnp.ones(valid.shape[:-1] + (1,), dtype=bool)], axis=-1)
    mask = slot_ok[..., None] & (dist >= 0)
    s = jnp.where(mask, s, -jnp.inf)
    p = jax.nn.softmax(s.reshape(s.shape[:3] + (-1,)), axis=-1).reshape(s.shape)
    return jnp.einsum('tkgjs,tkgjsd->tkgd', p.astype(vg.dtype), vg).astype(q.dtype)


def moba_layer(layer, hp, hs, cache_k, cache_v, page_table, w_qkv, g_q, g_k, w_o):
    b, s, _ = hp.shape
    db, t, _ = hs.shape
    qp, kp, vp = moba_project(hp, w_qkv, g_q, g_k)
    kb, vb, km = jax.vmap(moba_blocks)(kp, vp)
    n_c = s // MOBA_Q_CHUNK
    q_chunks = qp.reshape(b * n_c, MOBA_Q_CHUNK, MOBA_KV_HEADS, MOBA_GROUP, HEAD_DIM)
    pos_chunks = jnp.tile(jnp.arange(s, dtype=jnp.int32).reshape(n_c, MOBA_Q_CHUNK), (b, 1))
    seq_idx = jnp.repeat(jnp.arange(b, dtype=jnp.int32), n_c)

    def prompt_chunk(args):
        qc, pc, i = args
        return moba_attend(qc, pc, kb[i], vb[i], km[i])

    op = lax.map(prompt_chunk, (q_chunks, pos_chunks, seq_idx)).reshape(b, s, -1)
    qs, ks, vs = moba_project(hs, w_qkv, g_q, g_k)
    pos_s = PAST_LEN + jnp.arange(t, dtype=jnp.int32)

    def sample_seq(args):
        pages, qb, kn, vn = args
        k_all = jnp.concatenate([cache_k[layer, pages].reshape(-1, MOBA_KV_HEADS, HEAD_DIM), kn.astype(cache_k.dtype)], axis=0)
        v_all = jnp.concatenate([cache_v[layer, pages].reshape(-1, MOBA_KV_HEADS, HEAD_DIM), vn.astype(cache_v.dtype)], axis=0)
        kbs, vbs, kms = moba_blocks(k_all, v_all)
        return moba_attend(qb, pos_s, kbs, vbs, kms)

    o_s = lax.map(sample_seq, (page_table, qs, ks, vs)).reshape(db, t, -1)
    return (op @ w_o, o_s @ w_o, kp, vp, ks, vs)


def mla_project(h, pos, w_down, g_cq, w_uq, g_q, g_ckv, g_kpe):
    b, t, _ = h.shape
    d = h @ w_down
    cq = rmsnorm(d[..., :MLA_Q_LORA], g_cq)
    q = rmsnorm((cq @ w_uq).reshape(b, t, MLA_HEADS, MLA_QK_DIM), g_q)
    q_nope = q[..., :MLA_NOPE]
    q_pe = rope(q[..., MLA_NOPE:], pos)
    ckv = rmsnorm(d[..., MLA_Q_LORA:MLA_Q_LORA + MLA_KV_LORA], g_ckv)
    kpe = rope(rmsnorm(d[..., MLA_Q_LORA + MLA_KV_LORA:], g_kpe), pos)
    return q_nope, q_pe, ckv, kpe


def mla_layer(layer, hp, hs, pos_p, pos_s, cache_ckv, cache_kpe, page_table,
              w_down, g_cq, w_uq, g_q, g_ckv, g_kpe, w_uk, w_uv, w_o):
    b, s, _ = hp.shape
    db, t, _ = hs.shape
    scale = MLA_QK_DIM ** -0.5
    qn, qpe, ckv_p, kpe_p = mla_project(hp, pos_p, w_down, g_cq, w_uq, g_q, g_ckv, g_kpe)
    k_nope = jnp.einsum('bsc,chd->bshd', ckv_p, w_uk)
    v = jnp.einsum('bsc,chd->bshd', ckv_p, w_uv)
    n_qb = s // MLA_Q_BLOCK
    qn_b = qn.reshape(b, n_qb, MLA_Q_BLOCK, MLA_HEADS, MLA_NOPE).transpose(1, 0, 2, 3, 4)
    qp_b = qpe.reshape(b, n_qb, MLA_Q_BLOCK, MLA_HEADS, MLA_ROPE).transpose(1, 0, 2, 3, 4)
    qpos_b = pos_p.reshape(n_qb, MLA_Q_BLOCK)

    def prompt_block(args):
        qnb, qpb, qpos = args
        sc = (jnp.einsum('bqhd,bkhd->bhqk', qnb, k_nope).astype(jnp.float32)
              + jnp.einsum('bqhd,bkd->bhqk', qpb, kpe_p).astype(jnp.float32)) * scale
        sc = jnp.where(pos_p[None, None, None, :] <= qpos[None, None, :, None], sc, -jnp.inf)
        p = jax.nn.softmax(sc, axis=-1)
        return jnp.einsum('bhqk,bkhd->bqhd', p.astype(v.dtype), v)

    o_p = lax.map(prompt_block, (qn_b, qp_b, qpos_b)).transpose(1, 0, 2, 3, 4).reshape(b, s, -1)
    qn_s, qpe_s, ckv_s, kpe_s = mla_project(hs, pos_s, w_down, g_cq, w_uq, g_q, g_ckv, g_kpe)
    q_lat = jnp.einsum('bthd,chd->bthc', qn_s, w_uk)

    def sample_seq(args):
        pages, ql, qp_, cn, kn = args
        c_all = jnp.concatenate([cache_ckv[layer, pages].reshape(-1, MLA_KV_LORA), cn.astype(cache_ckv.dtype)], axis=0)
        k_all = jnp.concatenate([cache_kpe[layer, pages].reshape(-1, MLA_ROPE), kn.astype(cache_kpe.dtype)], axis=0)
        sc = (jnp.einsum('thc,lc->htl', ql, c_all).astype(jnp.float32)
              + jnp.einsum('thd,ld->htl', qp_, k_all).astype(jnp.float32)) * scale
        kpos = jnp.arange(c_all.shape[0], dtype=jnp.int32)
        sc = jnp.where(kpos[None, None, :] <= pos_s[None, :, None], sc, -jnp.inf)
        p = jax.nn.softmax(sc, axis=-1)
        return jnp.einsum('htl,lc->thc', p.astype(c_all.dtype), c_all)

    o_lat = lax.map(sample_seq, (page_table, q_lat, qpe_s, ckv_s, kpe_s))
    o_s = jnp.einsum('bthc,chd->bthd', o_lat, w_uv).reshape(db, t, -1)
    return (o_p @ w_o, o_s @ w_o, ckv_p, kpe_p, ckv_s, kpe_s)


def conv_ffn(h, conv_state, w_gate, w_up, conv_w, conv_b, w_down):
    t = h.shape[1]
    g = h @ w_gate
    u = h @ w_up
    gp = jnp.concatenate([conv_state.astype(g.dtype), g], axis=1)
    gc = conv_b
    for j in range(CONV_W):
        gc = gc + conv_w[j] * gp[:, j:j + t]
    return (jax.nn.silu(gc) * u) @ w_down, gp[:, t:]


def setup_inputs(seed: int = 0) -> dict:
    key = jax.random.key(seed)
    keys = iter(jax.random.split(key, 40))

    def nrm(shape, scale=1.0):
        x = jax.random.normal(next(keys), shape, jnp.float32)
        return x if scale == 1.0 else x * scale

    def gain(shape):
        return 1.0 + 0.05 * nrm(shape)

    n_pages = PAST_LEN // PAGE_SIZE
    n_used = DEC_BATCH * n_pages
    n_pool = n_used + (n_used + 3) // 4
    x_prompt = nrm((BATCH, SEQ, D_MODEL))
    x_sample = nrm((DEC_BATCH, DEC_SEQ, D_MODEL))
    cache_moba_k = nrm((N_MOBA_LAYERS, n_pool, PAGE_SIZE, MOBA_KV_HEADS, HEAD_DIM))
    cache_moba_v = nrm((N_MOBA_LAYERS, n_pool, PAGE_SIZE, MOBA_KV_HEADS, HEAD_DIM))
    cache_mla_ckv = nrm((N_MLA_LAYERS, n_pool, PAGE_SIZE, MLA_KV_LORA))
    cache_mla_kpe = nrm((N_MLA_LAYERS, n_pool, PAGE_SIZE, MLA_ROPE))
    state_ffn_conv = nrm((DEPTH, DEC_BATCH, CONV_W - 1, D_FF))
    page_table = jax.random.permutation(next(keys), n_pool)[:n_used].reshape(DEC_BATCH, n_pages).astype(jnp.int32)
    nqkv = (MOBA_Q_HEADS + 2 * MOBA_KV_HEADS) * HEAD_DIM
    ndown = MLA_Q_LORA + MLA_KV_LORA + MLA_ROPE
    return {
        'x_prompt': x_prompt,
        'x_sample': x_sample,
        'cache_moba_k': cache_moba_k,
        'cache_moba_v': cache_moba_v,
        'cache_mla_ckv': cache_mla_ckv,
        'cache_mla_kpe': cache_mla_kpe,
        'state_ffn_conv': state_ffn_conv,
        'page_table': page_table,
        'g_mix_norm': gain((DEPTH, D_MODEL)),
        'g_ffn_norm': gain((DEPTH, D_MODEL)),
        'moba_w_qkv': nrm((N_MOBA_LAYERS, D_MODEL, nqkv), D_MODEL ** -0.5),
        'moba_g_q': gain((N_MOBA_LAYERS, HEAD_DIM)),
        'moba_g_k': gain((N_MOBA_LAYERS, HEAD_DIM)),
        'moba_w_o': nrm((N_MOBA_LAYERS, MOBA_Q_HEADS * HEAD_DIM, D_MODEL), (MOBA_Q_HEADS * HEAD_DIM) ** -0.5),
        'mla_w_down': nrm((N_MLA_LAYERS, D_MODEL, ndown), D_MODEL ** -0.5),
        'mla_g_cq': gain((N_MLA_LAYERS, MLA_Q_LORA)),
        'mla_w_uq': nrm((N_MLA_LAYERS, MLA_Q_LORA, MLA_HEADS * MLA_QK_DIM), MLA_Q_LORA ** -0.5),
        'mla_g_q': gain((N_MLA_LAYERS, MLA_QK_DIM)),
        'mla_g_ckv': gain((N_MLA_LAYERS, MLA_KV_LORA)),
        'mla_g_kpe': gain((N_MLA_LAYERS, MLA_ROPE)),
        'mla_w_uk': nrm((N_MLA_LAYERS, MLA_KV_LORA, MLA_HEADS, MLA_NOPE), MLA_KV_LORA ** -0.5),
        'mla_w_uv': nrm((N_MLA_LAYERS, MLA_KV_LORA, MLA_HEADS, MLA_V_DIM), MLA_KV_LORA ** -0.5),
        'mla_w_o': nrm((N_MLA_LAYERS, MLA_HEADS * MLA_V_DIM, D_MODEL), (MLA_HEADS * MLA_V_DIM) ** -0.5),
        'ffn_w_gate': nrm((DEPTH, D_MODEL, D_FF), D_MODEL ** -0.5),
        'ffn_w_up': nrm((DEPTH, D_MODEL, D_FF), D_MODEL ** -0.5),
        'ffn_conv_w': nrm((DEPTH, CONV_W, D_FF), CONV_W ** -0.5),
        'ffn_conv_b': nrm((DEPTH, D_FF), 0.02),
        'ffn_w_down': nrm((DEPTH, D_FF, D_MODEL), D_FF ** -0.5),
    }


def reference(x_prompt, x_sample, cache_moba_k, cache_moba_v, cache_mla_ckv, cache_mla_kpe,
              state_ffn_conv, page_table, g_mix_norm, g_ffn_norm,
              moba_w_qkv, moba_g_q, moba_g_k, moba_w_o,
              mla_w_down, mla_g_cq, mla_w_uq, mla_g_q, mla_g_ckv, mla_g_kpe, mla_w_uk, mla_w_uv, mla_w_o,
              ffn_w_gate, ffn_w_up, ffn_conv_w, ffn_conv_b, ffn_w_down):
    xp, xs = x_prompt, x_sample
    b = xp.shape[0]
    pos_p = jnp.arange(xp.shape[1], dtype=jnp.int32)
    pos_s = PAST_LEN + jnp.arange(xs.shape[1], dtype=jnp.int32)
    mk_p, mv_p, mk_s, mv_s = [], [], [], []
    mc_p, mr_p, mc_s, mr_s = [], [], [], []
    cv_p, cv_s = [], []
    for i in range(DEPTH):
        hp = rmsnorm(xp, g_mix_norm[i])
        hs = rmsnorm(xs, g_mix_norm[i])
        j = i // N_MIXERS
        if i % N_MIXERS == 0:
            yp, ys, kp, vp, ks, vs = moba_layer(j, hp, hs, cache_moba_k, cache_moba_v, page_table,
                                                moba_w_qkv[j], moba_g_q[j], moba_g_k[j], moba_w_o[j])
            mk_p.append(kp); mv_p.append(vp); mk_s.append(ks); mv_s.append(vs)
        else:
            yp, ys, cp, rp, cs, rs = mla_layer(j, hp, hs, pos_p, pos_s, cache_mla_ckv, cache_mla_kpe, page_table,
                                               mla_w_down[j], mla_g_cq[j], mla_w_uq[j], mla_g_q[j], mla_g_ckv[j],
                                               mla_g_kpe[j], mla_w_uk[j], mla_w_uv[j], mla_w_o[j])
            mc_p.append(cp); mr_p.append(rp); mc_s.append(cs); mr_s.append(rs)
        xp = xp + yp
        xs = xs + ys
        hp = rmsnorm(xp, g_ffn_norm[i])
        hs = rmsnorm(xs, g_ffn_norm[i])
        zero_state = jnp.zeros((b, CONV_W - 1, D_FF), hp.dtype)
        fp, sp = conv_ffn(hp, zero_state, ffn_w_gate[i], ffn_w_up[i], ffn_conv_w[i], ffn_conv_b[i], ffn_w_down[i])
        fs, ss = conv_ffn(hs, state_ffn_conv[i], ffn_w_gate[i], ffn_w_up[i], ffn_conv_w[i], ffn_conv_b[i], ffn_w_down[i])
        cv_p.append(sp); cv_s.append(ss)
        xp = xp + fp
        xs = xs + fs
    return (xp, xs, jnp.stack(mk_p), jnp.stack(mv_p), jnp.stack(mk_s), jnp.stack(mv_s),
            jnp.stack(mc_p), jnp.stack(mr_p), jnp.stack(mc_s), jnp.stack(mr_s),
            jnp.stack(cv_p), jnp.stack(cv_s))
```

```python
import functools

import numpy as np
import jax
import jax.numpy as jnp
from jax import lax
from jax.experimental import pallas as pl
from jax.experimental.pallas import tpu as pltpu

F32 = jnp.float32
BF16 = jnp.bfloat16

D_MODEL = 4096
HEAD_DIM = 128
MOBA_Q_HEADS = 32
MOBA_KV_HEADS = 2
MOBA_GROUP = 16
MOBA_BLOCK = 256
MOBA_TOPK = 3
MLA_HEADS = 32
MLA_Q_LORA = 1024
MLA_KV_LORA = 512
MLA_NOPE = 128
MLA_ROPE = 64
MLA_QK_DIM = MLA_NOPE + MLA_ROPE
MLA_V_DIM = 128
ROPE_THETA = 10000.0
CONV_W = 3
NORM_EPS = 1e-6
PAGE_SIZE = 128

NEG = -1e30
VMEM_LIMIT = 56 * 1024 * 1024
NT = (((1,), (1,)), ((), ()))


def _params(sem):
    return pltpu.CompilerParams(dimension_semantics=sem, vmem_limit_bytes=VMEM_LIMIT)


def _rmsnorm_kernel(x_ref, g_ref, o_ref):
    x = x_ref[...]
    y = x * lax.rsqrt(jnp.mean(x * x, axis=-1, keepdims=True) + NORM_EPS)
    o_ref[...] = (y * g_ref[...]).astype(o_ref.dtype)


def rmsnorm_rows(x, g, tr=256):
    m, d = x.shape
    return pl.pallas_call(
        _rmsnorm_kernel,
        grid=(m // tr,),
        in_specs=[pl.BlockSpec((tr, d), lambda i: (i, 0)),
                  pl.BlockSpec((1, d), lambda i: (0, 0))],
        out_specs=pl.BlockSpec((tr, d), lambda i: (i, 0)),
        out_shape=jax.ShapeDtypeStruct((m, d), BF16),
        compiler_params=_params(("parallel",)),
        name="rmsnorm",
    )(x, g.reshape(1, d))


def _mm_kernel(a_ref, w_ref, *rest, has_res):
    o_ref = rest[-1]
    acc = jnp.dot(a_ref[...].astype(BF16), w_ref[...].astype(BF16), preferred_element_type=F32)
    if has_res:
        acc = rest[0][...] + acc
    o_ref[...] = acc.astype(o_ref.dtype)


def matmul(a, w, res=None, *, tm, tn, out_dtype=F32, name="matmul"):
    m, k = a.shape
    n = w.shape[1]
    assert m % tm == 0 and n % tn == 0, (m, n, tm, tn)
    in_specs = [pl.BlockSpec((tm, k), lambda i, j: (i, 0)),
                pl.BlockSpec((k, tn), lambda i, j: (0, j))]
    args = [a, w]
    if res is not None:
        in_specs.append(pl.BlockSpec((tm, tn), lambda i, j: (i, j)))
        args.append(res)
    return pl.pallas_call(
        functools.partial(_mm_kernel, has_res=res is not None),
        grid=(m // tm, n // tn),
        in_specs=in_specs,
        out_specs=pl.BlockSpec((tm, tn), lambda i, j: (i, j)),
        out_shape=jax.ShapeDtypeStruct((m, n), out_dtype),
        compiler_params=_params(("parallel", "parallel")),
        name=name,
    )(*args)


def _hmm_kernel(a_ref, w_ref, o_ref):
    o_ref[...] = jnp.dot(a_ref[...].astype(BF16), w_ref[...].astype(BF16),
                         preferred_element_type=F32).astype(o_ref.dtype)


def head_matmul(a, w, *, out_dtype, name):
    m = a.shape[0]
    nh, ka, n = w.shape
    return pl.pallas_call(
        _hmm_kernel,
        grid=(nh,),
        in_specs=[pl.BlockSpec((m, ka), lambda h: (0, h)),
                  pl.BlockSpec((None, ka, n), lambda h: (h, 0, 0))],
        out_specs=pl.BlockSpec((m, n), lambda h: (0, h)),
        out_shape=jax.ShapeDtypeStruct((m, nh * n), out_dtype),
        compiler_params=_params(("parallel",)),
        name=name,
    )(a, w)


def _silu_mul(gc, u):
    return (gc * (1.0 / (1.0 + jnp.exp(-gc)))) * u


def _ffn_up_prompt_kernel(a_ref, wg_ref, wu_ref, cw_ref, cb_ref, st_ref, act_ref, ns_ref):
    a = a_ref[...]
    g = jnp.dot(a, wg_ref[...].astype(BF16), preferred_element_type=F32)
    u = jnp.dot(a, wu_ref[...].astype(BF16), preferred_element_type=F32)
    tm = g.shape[0]
    row = lax.broadcasted_iota(jnp.int32, g.shape, 0)
    st0 = st_ref[0, 0:1, :]
    st1 = st_ref[0, 1:2, :]
    g1 = jnp.where(row >= 1, pltpu.roll(g, 1, 0), st1)
    g2 = jnp.where(row >= 2, pltpu.roll(g, 2, 0), jnp.where(row == 1, st1, st0))
    cw = cw_ref[...]
    gc = cb_ref[...] + cw[0:1, :] * g2
    gc = gc + cw[1:2, :] * g1
    gc = gc + cw[2:3, :] * g
    act_ref[...] = _silu_mul(gc, u).astype(act_ref.dtype)
    ns_ref[0] = g[tm - 2:tm, :]


def ffn_up_prompt(h, wg, wu, cw, cb, state, *, seq, tn=256):
    m, k = h.shape
    n = wg.shape[1]
    nb = m // seq
    return pl.pallas_call(
        _ffn_up_prompt_kernel,
        grid=(nb, n // tn),
        in_specs=[pl.BlockSpec((seq, k), lambda i, j: (i, 0), pipeline_mode=pl.Buffered(1)),
                  pl.BlockSpec((k, tn), lambda i, j: (0, j)),
                  pl.BlockSpec((k, tn), lambda i, j: (0, j)),
                  pl.BlockSpec((CONV_W, tn), lambda i, j: (0, j)),
                  pl.BlockSpec((1, tn), lambda i, j: (0, j)),
                  pl.BlockSpec((1, 2, tn), lambda i, j: (i, 0, j))],
        out_specs=[pl.BlockSpec((seq, tn), lambda i, j: (i, j)),
                   pl.BlockSpec((1, 2, tn), lambda i, j: (i, 0, j))],
        out_shape=[jax.ShapeDtypeStruct((m, n), BF16),
                   jax.ShapeDtypeStruct((nb, 2, n), F32)],
        compiler_params=_params(("parallel", "parallel")),
        name="ffn_up_prompt",
    )(h, wg, wu, cw, cb.reshape(1, n), state)


def _ffn_up_sample_kernel(a_ref, wg_ref, wu_ref, cw_ref, cb_ref, st_ref, act_ref, ns_ref, *, nseq):
    a = a_ref[...]
    g = jnp.dot(a, wg_ref[...].astype(BF16), preferred_element_type=F32)
    u = jnp.dot(a, wu_ref[...].astype(BF16), preferred_element_type=F32)
    tm = g.shape[0]
    st0 = st_ref[0]
    st1 = st_ref[1]
    g1 = jnp.concatenate([st1, g[:tm - nseq]], axis=0)
    g2 = jnp.concatenate([st0, st1, g[:tm - 2 * nseq]], axis=0)
    cw = cw_ref[...]
    gc = cb_ref[...] + cw[0:1, :] * g2
    gc = gc + cw[1:2, :] * g1
    gc = gc + cw[2:3, :] * g
    act_ref[...] = _silu_mul(gc, u).astype(act_ref.dtype)
    ns_ref[0] = g[tm - 2 * nseq:tm - nseq]
    ns_ref[1] = g[tm - nseq:]


def ffn_up_sample(h, wg, wu, cw, cb, state_t, *, nseq, tn=256):
    m, k = h.shape
    n = wg.shape[1]
    return pl.pallas_call(
        functools.partial(_ffn_up_sample_kernel, nseq=nseq),
        grid=(n // tn,),
        in_specs=[pl.BlockSpec((m, k), lambda j: (0, 0)),
                  pl.BlockSpec((k, tn), lambda j: (0, j)),
                  pl.BlockSpec((k, tn), lambda j: (0, j)),
                  pl.BlockSpec((CONV_W, tn), lambda j: (0, j)),
                  pl.BlockSpec((1, tn), lambda j: (0, j)),
                  pl.BlockSpec((2, nseq, tn), lambda j: (0, 0, j))],
        out_specs=[pl.BlockSpec((m, tn), lambda j: (0, j)),
                   pl.BlockSpec((2, nseq, tn), lambda j: (0, 0, j))],
        out_shape=[jax.ShapeDtypeStruct((m, n), BF16),
                   jax.ShapeDtypeStruct((2, nseq, n), F32)],
        compiler_params=_params(("parallel",)),
        name="ffn_up_sample",
    )(h, wg, wu, cw, cb.reshape(1, n), state_t)


def _moba_qknorm_kernel(x_ref, gq_ref, gk_ref, q_ref, k_ref, km_ref):
    gq = gq_ref[...]
    gk = gk_ref[...]
    for h in range(MOBA_Q_HEADS):
        x = x_ref[:, h * HEAD_DIM:(h + 1) * HEAD_DIM]
        y = x * lax.rsqrt(jnp.mean(x * x, axis=-1, keepdims=True) + NORM_EPS)
        q_ref[:, h * HEAD_DIM:(h + 1) * HEAD_DIM] = y * gq
    for j in range(MOBA_KV_HEADS):
        c0 = (MOBA_Q_HEADS + j) * HEAD_DIM
        x = x_ref[:, c0:c0 + HEAD_DIM]
        y = (x * lax.rsqrt(jnp.mean(x * x, axis=-1, keepdims=True) + NORM_EPS)) * gk
        k_ref[:, j * HEAD_DIM:(j + 1) * HEAD_DIM] = y
        km_ref[0, :, j * HEAD_DIM:(j + 1) * HEAD_DIM] = jnp.mean(y, axis=0, keepdims=True)


def moba_qknorm(qkv, gq, gk):
    m, n = qkv.shape
    tr = MOBA_BLOCK
    nq = MOBA_Q_HEADS * HEAD_DIM
    nk = MOBA_KV_HEADS * HEAD_DIM
    return pl.pallas_call(
        _moba_qknorm_kernel,
        grid=(m // tr,),
        in_specs=[pl.BlockSpec((tr, n), lambda i: (i, 0)),
                  pl.BlockSpec((1, HEAD_DIM), lambda i: (0, 0)),
                  pl.BlockSpec((1, HEAD_DIM), lambda i: (0, 0))],
        out_specs=[pl.BlockSpec((tr, nq), lambda i: (i, 0)),
                   pl.BlockSpec((tr, nk), lambda i: (i, 0)),
                   pl.BlockSpec((1, 1, nk), lambda i: (i, 0, 0))],
        out_shape=[jax.ShapeDtypeStruct((m, nq), F32),
                   jax.ShapeDtypeStruct((m, nk), F32),
                   jax.ShapeDtypeStruct((m // tr, 1, nk), F32)],
        compiler_params=_params(("parallel",)),
        name="moba_qknorm",
    )(qkv, gq.reshape(1, HEAD_DIM), gk.reshape(1, HEAD_DIM))


def _top3(gate, allowed):
    nb = gate.shape[1]
    lane = lax.broadcasted_iota(jnp.int32, gate.shape, 1)
    gm = jnp.where(allowed, gate, -jnp.inf)
    sel = jnp.zeros(gate.shape, F32)
    for _ in range(MOBA_TOPK):
        mx = jnp.max(gm, axis=1, keepdims=True)
        idx = jnp.min(jnp.where(gm == mx, lane, nb), axis=1, keepdims=True)
        pick = lane == idx
        sel = jnp.where(pick, 1.0, sel)
        gm = jnp.where(pick, -jnp.inf, gm)
    return jnp.where(allowed, sel, 0.0)


def _online_update(carry, s, ok, v):
    m, l, acc = carry
    s = jnp.where(ok, s, NEG)
    m_new = jnp.maximum(m, jnp.max(s, axis=1, keepdims=True))
    p = jnp.where(ok, jnp.exp(s - m_new), 0.0)
    alpha = jnp.exp(m - m_new)
    l = alpha * l + jnp.sum(p, axis=1, keepdims=True)
    acc = alpha * acc + jnp.dot(p.astype(BF16), v, preferred_element_type=F32)
    return m_new, l, acc


def _moba_prompt_kernel(slopes_ref, q_ref, k_ref, v_ref, km_ref, o_ref):
    kvh = pl.program_id(1)
    qt = pl.program_id(2)
    g = pl.program_id(3)
    slope = slopes_ref[kvh * MOBA_GROUP + g]
    scale = HEAD_DIM ** -0.5
    blk = MOBA_BLOCK
    q = q_ref[...]
    km = km_ref[0, 0]
    gate = lax.dot_general(q, km, NT, precision=lax.Precision.HIGHEST, preferred_element_type=F32)
    lane = lax.broadcasted_iota(jnp.int32, gate.shape, 1)
    sel = _top3(gate, lane < qt)
    qb = q.astype(BF16)
    rc = (lax.broadcasted_iota(jnp.int32, (blk, blk), 0)
          - lax.broadcasted_iota(jnp.int32, (blk, blk), 1))

    def scores(n):
        off = pl.multiple_of(n * blk, blk)
        kb = k_ref[pl.ds(off, blk), :].astype(BF16)
        vb = v_ref[pl.ds(off, blk), :].astype(BF16)
        s = lax.dot_general(qb, kb, NT, preferred_element_type=F32) * scale
        dist = (qt - n) * blk + rc
        return s - slope * dist.astype(F32), dist, vb

    def body(n, carry):
        s, _, vb = scores(n)
        col = jnp.max(jnp.where(lane == n, sel, 0.0), axis=1, keepdims=True)
        ok = jnp.broadcast_to(col > 0.5, s.shape)
        return _online_update(carry, s, ok, vb)

    init = (jnp.full((blk, 1), NEG, F32), jnp.zeros((blk, 1), F32), jnp.zeros((blk, HEAD_DIM), F32))
    carry = lax.fori_loop(0, qt, body, init)
    s, dist, vb = scores(qt)
    m, l, acc = _online_update(carry, s, dist >= 0, vb)
    o_ref[...] = (acc / l).astype(o_ref.dtype)


def moba_prompt_attention(qn, kn, qkv, kmean, slopes, *, nbatch, seq):
    m = qn.shape[0]
    nqt = seq // MOBA_BLOCK
    vcol0 = MOBA_Q_HEADS + MOBA_KV_HEADS
    return pl.pallas_call(
        _moba_prompt_kernel,
        grid=(nbatch, MOBA_KV_HEADS, nqt, MOBA_GROUP),
        in_specs=[pl.BlockSpec(memory_space=pltpu.SMEM),
                  pl.BlockSpec((MOBA_BLOCK, HEAD_DIM), lambda b, k, t, g: (b * nqt + t, k * MOBA_GROUP + g)),
                  pl.BlockSpec((seq, HEAD_DIM), lambda b, k, t, g: (b, k)),
                  pl.BlockSpec((seq, HEAD_DIM), lambda b, k, t, g: (b, vcol0 + k)),
                  pl.BlockSpec((1, 1, nqt, HEAD_DIM), lambda b, k, t, g: (b, k, 0, 0))],
        out_specs=pl.BlockSpec((MOBA_BLOCK, HEAD_DIM), lambda b, k, t, g: (b * nqt + t, k * MOBA_GROUP + g)),
        out_shape=jax.ShapeDtypeStruct((m, MOBA_Q_HEADS * HEAD_DIM), BF16),
        compiler_params=_params(("parallel", "parallel", "parallel", "parallel")),
        name="moba_prompt_attn",
    )(slopes, qn, kn, qkv, kmean)


def _kv_chunk(page_refs):
    pages = [jnp.concatenate([r[pl.ds(j, PAGE_SIZE, stride=MOBA_KV_HEADS), :] for j in range(MOBA_KV_HEADS)], axis=1)
             for r in page_refs]
    return jnp.concatenate(pages, axis=0)


def _moba_sample_kernel(pt_ref, *refs, pps, nchunk, past_len):
    del pt_ref
    k_refs = refs[:pps]
    v_refs = refs[pps:2 * pps]
    q_ref, kn_ref, vn_ref, slope_ref, tpos_ref, o_ref = refs[2 * pps:2 * pps + 6]
    s_ref, km_ref, sel_ref, m_ref, l_ref, acc_ref = refs[2 * pps + 6:]
    c = pl.program_id(1)
    scale = HEAD_DIM ** -0.5
    ck = pps * PAGE_SIZE
    bpc = ck // MOBA_BLOCK
    nblk = nchunk * bpc
    slope = slope_ref[...]
    tpos = tpos_ref[...]

    @pl.when(c < nchunk)
    def _scores():
        kc = _kv_chunk(k_refs)
        s = lax.dot_general(q_ref[0].astype(BF16), kc.astype(BF16), NT, preferred_element_type=F32) * scale
        kpos = (c * ck + lax.broadcasted_iota(jnp.int32, s.shape, 1)).astype(F32)
        s_ref[c] = s - slope * (tpos - kpos)
        km_ref[c] = jnp.sum(kc.reshape(bpc, MOBA_BLOCK, kc.shape[1]), axis=1) * (1.0 / MOBA_BLOCK)

    @pl.when(c == nchunk)
    def _select():
        km = km_ref[...].reshape(nblk, km_ref.shape[2])
        gate = lax.dot_general(q_ref[0], km, NT, precision=lax.Precision.HIGHEST,
                               preferred_element_type=F32)
        sel_ref[...] = _top3(gate, jnp.full(gate.shape, True))
        m_ref[...] = jnp.full(m_ref.shape, NEG, F32)
        l_ref[...] = jnp.zeros(l_ref.shape, F32)
        acc_ref[...] = jnp.zeros(acc_ref.shape, F32)

    @pl.when(c >= nchunk)
    def _attend():
        cc = c - nchunk
        s = s_ref[cc]
        brow = lax.broadcasted_iota(jnp.int32, (nblk, ck), 0)
        bcol = cc * bpc + lax.broadcasted_iota(jnp.int32, (nblk, ck), 1) // MOBA_BLOCK
        expand = jnp.where(brow == bcol, 1.0, 0.0).astype(BF16)
        ok = jnp.dot(sel_ref[...].astype(BF16), expand, preferred_element_type=F32) > 0.5
        vc = _kv_chunk(v_refs).astype(BF16)
        m, l, acc = _online_update((m_ref[...], l_ref[...], acc_ref[...]), s, ok, vc)
        m_ref[...] = m
        l_ref[...] = l
        acc_ref[...] = acc

    @pl.when(c == 2 * nchunk - 1)
    def _own_block():
        kn = kn_ref[0]
        s = lax.dot_general(q_ref[0].astype(BF16), kn.astype(BF16), NT, preferred_element_type=F32) * scale
        j = lax.broadcasted_iota(jnp.int32, s.shape, 1).astype(F32)
        dist = tpos - (past_len + j)
        s = s - slope * dist
        m, l, acc = _online_update((m_ref[...], l_ref[...], acc_ref[...]), s, dist >= 0,
                                   vn_ref[0].astype(BF16))
        o_ref[0] = acc / l


def moba_sample_attention(page_table, cache_k, cache_v, q_bd, k_new, v_new, slope_rows, tpos_rows,
                          *, past_len, pps=16):
    ns, r, w = q_bd.shape
    assert past_len % MOBA_BLOCK == 0 and (pps * PAGE_SIZE) % MOBA_BLOCK == 0
    npages = past_len // PAGE_SIZE
    nchunk = npages // pps
    prows = PAGE_SIZE * MOBA_KV_HEADS
    kspecs = [pl.BlockSpec((None, prows, HEAD_DIM),
                           lambda s, c, pt, j=j: (pt[s * npages + jnp.minimum(c, nchunk - 1) * pps + j], 0, 0))
              for j in range(pps)]
    vspecs = [pl.BlockSpec((None, prows, HEAD_DIM),
                           lambda s, c, pt, j=j: (pt[s * npages + jnp.maximum(c - nchunk, 0) * pps + j], 0, 0))
              for j in range(pps)]
    per_seq = lambda s, c, pt: (s, 0, 0)
    const2 = lambda s, c, pt: (0, 0)
    nblk = past_len // MOBA_BLOCK
    grid_spec = pltpu.PrefetchScalarGridSpec(
        num_scalar_prefetch=1,
        grid=(ns, 2 * nchunk),
        in_specs=kspecs + vspecs + [
            pl.BlockSpec((1, r, w), per_seq),
            pl.BlockSpec((1, 8, w), per_seq),
            pl.BlockSpec((1, 8, w), per_seq),
            pl.BlockSpec((r, 1), const2),
            pl.BlockSpec((r, 1), const2)],
        out_specs=pl.BlockSpec((1, r, w), per_seq),
        scratch_shapes=[pltpu.VMEM((nchunk, r, pps * PAGE_SIZE), F32),
                        pltpu.VMEM((nchunk, pps * PAGE_SIZE // MOBA_BLOCK, w), F32),
                        pltpu.VMEM((r, nblk), F32),
                        pltpu.VMEM((r, 1), F32),
                        pltpu.VMEM((r, 1), F32),
                        pltpu.VMEM((r, w), F32)])
    return pl.pallas_call(
        functools.partial(_moba_sample_kernel, pps=pps, nchunk=nchunk, past_len=past_len),
        grid_spec=grid_spec,
        out_shape=jax.ShapeDtypeStruct((ns, r, w), F32),
        compiler_params=_params(("parallel", "arbitrary")),
        name="moba_sample_attn",
    )(page_table.reshape(-1), *([cache_k] * pps), *([cache_v] * pps),
      q_bd, k_new, v_new, slope_rows, tpos_rows)


def _mla_kv_prep_kernel(d_ref, gcq_ref, gckv_ref, gk_ref, gks_ref, cos_ref, sin_ref,
                        cq_ref, ckv_ref, ckvb_ref, kpe_ref):
    c0 = MLA_Q_LORA
    c1 = c0 + MLA_KV_LORA
    x = d_ref[:, 0:c0]
    y = x * lax.rsqrt(jnp.mean(x * x, axis=-1, keepdims=True) + NORM_EPS)
    cq_ref[...] = (y * gcq_ref[...]).astype(cq_ref.dtype)
    x = d_ref[:, c0:c1]
    y = (x * lax.rsqrt(jnp.mean(x * x, axis=-1, keepdims=True) + NORM_EPS)) * gckv_ref[...]
    ckv_ref[...] = y
    ckvb_ref[...] = y.astype(ckvb_ref.dtype)
    x = d_ref[:, c1:c1 + 128][:, :MLA_ROPE]
    xs = d_ref[:, c1 + 128:c1 + 256][:, :MLA_ROPE]
    rstd = lax.rsqrt(jnp.mean(x * x, axis=-1, keepdims=True) + NORM_EPS)
    kpe_ref[...] = ((x * rstd) * gk_ref[...]) * cos_ref[...] + ((xs * rstd) * gks_ref[...]) * sin_ref[...]


def mla_kv_prep(d, g_cq, g_ckv, g_kpe, g_kpe_sw, cos2, sin2s, *, tr=256):
    m, n = d.shape
    tab_rows = cos2.shape[0]
    ntab = tab_rows // tr if tab_rows >= tr else 1
    row = lambda i: (i, 0)
    const = lambda i: (0, 0)
    tab = lambda i: (i % ntab, 0)
    return pl.pallas_call(
        _mla_kv_prep_kernel,
        grid=(m // tr,),
        in_specs=[pl.BlockSpec((tr, n), row),
                  pl.BlockSpec((1, MLA_Q_LORA), const),
                  pl.BlockSpec((1, MLA_KV_LORA), const),
                  pl.BlockSpec((1, MLA_ROPE), const),
                  pl.BlockSpec((1, MLA_ROPE), const),
                  pl.BlockSpec((tr, MLA_ROPE), tab),
                  pl.BlockSpec((tr, MLA_ROPE), tab)],
        out_specs=[pl.BlockSpec((tr, MLA_Q_LORA), row),
                   pl.BlockSpec((tr, MLA_KV_LORA), row),
                   pl.BlockSpec((tr, MLA_KV_LORA), row),
                   pl.BlockSpec((tr, MLA_ROPE), row)],
        out_shape=[jax.ShapeDtypeStruct((m, MLA_Q_LORA), BF16),
                   jax.ShapeDtypeStruct((m, MLA_KV_LORA), F32),
                   jax.ShapeDtypeStruct((m, MLA_KV_LORA), BF16),
                   jax.ShapeDtypeStruct((m, MLA_ROPE), F32)],
        compiler_params=_params(("parallel",)),
        name="mla_kv_prep",
    )(d, g_cq.reshape(1, -1), g_ckv.reshape(1, -1), g_kpe.reshape(1, -1), g_kpe_sw.reshape(1, -1), cos2, sin2s)


def _mla_q_prep_kernel(x_ref, gn_ref, gr_ref, grs_ref, cos_ref, sin_ref, qn_ref, qpe_ref):
    nn = MLA_HEADS * MLA_NOPE
    nr = MLA_HEADS * MLA_ROPE
    gn = gn_ref[...]
    gr = gr_ref[...]
    grs = grs_ref[...]
    cos = cos_ref[...]
    sin = sin_ref[...]
    lane = lax.broadcasted_iota(jnp.int32, (x_ref.shape[0], 128), 1)
    first = lane < MLA_ROPE
    for hp in range(MLA_HEADS // 2):
        r = x_ref[:, nn + hp * 128:nn + (hp + 1) * 128]
        rs = x_ref[:, nn + nr + hp * 128:nn + nr + (hp + 1) * 128]
        r2 = r * r
        ss_a = jnp.sum(jnp.where(first, r2, 0.0), axis=-1, keepdims=True)
        ss_b = jnp.sum(jnp.where(first, 0.0, r2), axis=-1, keepdims=True)
        rstd_pair = []
        for sub, ss_r in ((0, ss_a), (1, ss_b)):
            h = 2 * hp + sub
            n = x_ref[:, h * MLA_NOPE:(h + 1) * MLA_NOPE]
            ss = jnp.sum(n * n, axis=-1, keepdims=True) + ss_r
            rstd = lax.rsqrt(ss * (1.0 / MLA_QK_DIM) + NORM_EPS)
            qn_ref[:, h * MLA_NOPE:(h + 1) * MLA_NOPE] = ((n * rstd) * gn).astype(qn_ref.dtype)
            rstd_pair.append(rstd)
        rstd = jnp.where(first, rstd_pair[0], rstd_pair[1])
        qpe = ((r * rstd) * gr) * cos + ((rs * rstd) * grs) * sin
        qpe_ref[:, hp * 128:(hp + 1) * 128] = qpe.astype(qpe_ref.dtype)


def mla_q_prep(x, g_n, g_r2, g_rs2, cos4, sin4s, *, tr=256):
    m, n = x.shape
    tab_rows = cos4.shape[0]
    ntab = tab_rows // tr if tab_rows >= tr else 1
    row = lambda i: (i, 0)
    const = lambda i: (0, 0)
    tab = lambda i: (i % ntab, 0)
    return pl.pallas_call(
        _mla_q_prep_kernel,
        grid=(m // tr,),
        in_specs=[pl.BlockSpec((tr, n), row),
                  pl.BlockSpec((1, MLA_NOPE), const),
                  pl.BlockSpec((1, 128), const),
                  pl.BlockSpec((1, 128), const),
                  pl.BlockSpec((tr, 128), tab),
                  pl.BlockSpec((tr, 128), tab)],
        out_specs=[pl.BlockSpec((tr, MLA_HEADS * MLA_NOPE), row),
                   pl.BlockSpec((tr, MLA_HEADS * MLA_ROPE), row)],
        out_shape=[jax.ShapeDtypeStruct((m, MLA_HEADS * MLA_NOPE), BF16),
                   jax.ShapeDtypeStruct((m, MLA_HEADS * MLA_ROPE), BF16)],
        compiler_params=_params(("parallel",)),
        name="mla_q_prep",
    )(x, g_n.reshape(1, -1), g_r2.reshape(1, -1), g_rs2.reshape(1, -1), cos4, sin4s)


def _mla_prompt_kernel(qn_ref, qpe_ref, kn_ref, kpe_ref, v_ref, o_ref, *, tq):
    h = pl.program_id(1)
    qt = pl.program_id(2)
    scale = MLA_QK_DIM ** -0.5
    lane = lax.broadcasted_iota(jnp.int32, (tq, 128), 1)
    mine = (lane // MLA_ROPE) == (h % 2)
    qpe = jnp.where(mine, qpe_ref[...].astype(F32), 0.0).astype(BF16)
    q = jnp.concatenate([qn_ref[...], qpe], axis=1)
    rc = (lax.broadcasted_iota(jnp.int32, (tq, tq), 0) - lax.broadcasted_iota(jnp.int32, (tq, tq), 1))

    def step(n, carry, ok):
        off = pl.multiple_of(n * tq, tq)
        k = jnp.concatenate([kn_ref[pl.ds(off, tq), :], kpe_ref[pl.ds(off, tq), :]], axis=1)
        s = lax.dot_general(q, k, NT, preferred_element_type=F32) * scale
        return _online_update(carry, s, ok, v_ref[pl.ds(off, tq), :])

    init = (jnp.full((tq, 1), NEG, F32), jnp.zeros((tq, 1), F32), jnp.zeros((tq, MLA_V_DIM), F32))
    carry = lax.fori_loop(0, qt, lambda n, cr: step(n, cr, jnp.full((tq, tq), True)), init)
    m, l, acc = step(qt, carry, rc >= 0)
    o_ref[...] = (acc / l).astype(o_ref.dtype)


def mla_prompt_attention(qn, qpe, kv, kpe, *, nbatch, seq, tq=256):
    m = qn.shape[0]
    nqt = seq // tq
    kpe = jnp.concatenate([kpe, kpe], axis=1).astype(BF16)
    return pl.pallas_call(
        functools.partial(_mla_prompt_kernel, tq=tq),
        grid=(nbatch, MLA_HEADS, nqt),
        in_specs=[pl.BlockSpec((tq, MLA_NOPE), lambda b, h, t: (b * nqt + t, h)),
                  pl.BlockSpec((tq, 128), lambda b, h, t: (b * nqt + t, h // 2)),
                  pl.BlockSpec((seq, MLA_NOPE), lambda b, h, t: (b, h)),
                  pl.BlockSpec((seq, 2 * MLA_ROPE), lambda b, h, t: (b, 0)),
                  pl.BlockSpec((seq, MLA_V_DIM), lambda b, h, t: (b, MLA_HEADS + h))],
        out_specs=pl.BlockSpec((tq, MLA_V_DIM), lambda b, h, t: (b * nqt + t, h)),
        out_shape=jax.ShapeDtypeStruct((m, MLA_HEADS * MLA_V_DIM), BF16),
        compiler_params=_params(("parallel", "parallel", "parallel")),
        name="mla_prompt_attn",
    )(qn, qpe, kv, kpe, kv)


def _mla_sample_kernel(pt_ref, *refs, pps, nchunk, past_len):
    del pt_ref
    c_refs = refs[:pps]
    r_refs = refs[pps:2 * pps]
    ql_ref, qp_ref, cn_ref, rn_ref, tpos_ref, o_ref, m_ref, l_ref, acc_ref = refs[2 * pps:]
    c = pl.program_id(1)
    scale = MLA_QK_DIM ** -0.5
    ql = ql_ref[0]
    qp = qp_ref[0]

    @pl.when(c == 0)
    def _init():
        m_ref[...] = jnp.full(m_ref.shape, NEG, F32)
        l_ref[...] = jnp.zeros(l_ref.shape, F32)
        acc_ref[...] = jnp.zeros(acc_ref.shape, F32)

    cc = jnp.concatenate([r[...] for r in c_refs], axis=0).astype(BF16)
    rr = jnp.concatenate([r[...] for r in r_refs], axis=1).astype(BF16)
    s = (lax.dot_general(ql, cc, NT, preferred_element_type=F32)
         + jnp.dot(qp, rr, preferred_element_type=F32)) * scale
    m, l, acc = _online_update((m_ref[...], l_ref[...], acc_ref[...]), s, jnp.full(s.shape, True), cc)
    m_ref[...] = m
    l_ref[...] = l
    acc_ref[...] = acc

    @pl.when(c == nchunk - 1)
    def _new_rows():
        cn = cn_ref[0].astype(BF16)
        rn = rn_ref[0].astype(BF16)
        s2 = (lax.dot_general(ql, cn, NT, preferred_element_type=F32)
              + lax.dot_general(qp, rn, NT, preferred_element_type=F32)) * scale
        j = lax.broadcasted_iota(jnp.int32, s2.shape, 1).astype(F32)
        ok = (past_len + j) <= tpos_ref[...]
        m2, l2, acc2 = _online_update((m_ref[...], l_ref[...], acc_ref[...]), s2, ok, cn)
        o_ref[0] = (acc2 / l2).astype(o_ref.dtype)


def mla_sample_attention(page_table, cache_ckv, cache_kpe, q_lat, q_pe, c_new, r_new, tpos_rows,
                         *, past_len, pps=16):
    ns, r, _ = q_lat.shape
    npages = past_len // PAGE_SIZE
    nchunk = npages // pps
    page = lambda s, c, pt, j=0: (pt[s * npages + c * pps + j], 0, 0)
    cspecs = [pl.BlockSpec((None, PAGE_SIZE, MLA_KV_LORA), functools.partial(page, j=j)) for j in range(pps)]
    rspecs = [pl.BlockSpec((None, MLA_ROPE, PAGE_SIZE), functools.partial(page, j=j)) for j in range(pps)]
    per_seq = lambda s, c, pt: (s, 0, 0)
    grid_spec = pltpu.PrefetchScalarGridSpec(
        num_scalar_prefetch=1,
        grid=(ns, nchunk),
        in_specs=cspecs + rspecs + [
            pl.BlockSpec((1, r, MLA_KV_LORA), per_seq),
            pl.BlockSpec((1, r, MLA_ROPE), per_seq),
            pl.BlockSpec((1, 8, MLA_KV_LORA), per_seq),
            pl.BlockSpec((1, 8, MLA_ROPE), per_seq),
            pl.BlockSpec((r, 1), lambda s, c, pt: (0, 0))],
        out_specs=pl.BlockSpec((1, r, MLA_KV_LORA), per_seq),
        scratch_shapes=[pltpu.VMEM((r, 1), F32), pltpu.VMEM((r, 1), F32), pltpu.VMEM((r, MLA_KV_LORA), F32)])
    return pl.pallas_call(
        functools.partial(_mla_sample_kernel, pps=pps, nchunk=nchunk, past_len=past_len),
        grid_spec=grid_spec,
        out_shape=jax.ShapeDtypeStruct((ns, r, MLA_KV_LORA), BF16),
        compiler_params=_params(("parallel", "arbitrary")),
        name="mla_sample_attn",
    )(page_table.reshape(-1), *([cache_ckv] * pps), *([cache_kpe] * pps),
      q_lat, q_pe, c_new, r_new, tpos_rows)


def _rope_tables(pos, reps):
    half = MLA_ROPE // 2
    inv = ROPE_THETA ** (-jnp.arange(half, dtype=F32) / half)
    ang = pos.astype(F32)[:, None] * inv[None, :]
    cos, sin = jnp.cos(ang), jnp.sin(ang)
    cos2 = jnp.concatenate([cos, cos], axis=1)
    sin2 = jnp.concatenate([-sin, sin], axis=1)
    return jnp.tile(cos2, (1, reps)), jnp.tile(sin2, (1, reps))


def _swap_halves(x):
    half = x.shape[-1] // 2
    return jnp.concatenate([x[..., half:], x[..., :half]], axis=-1)


def _pad_rows8(x, ns):
    t = x.shape[0] // ns
    y = x.reshape(t, ns, x.shape[1]).transpose(1, 0, 2)
    return jnp.pad(y, ((0, 0), (0, 8 - t), (0, 0)))


def _moba_layer(hp, hs, xp, xs, cache_k, cache_v, page_table, w_qkv, g_q, g_k, w_o, *, nbatch, seq, ns, t, past_len):
    nq = MOBA_Q_HEADS * HEAD_DIM
    nk = MOBA_KV_HEADS * HEAD_DIM
    slopes = jnp.exp2(-8.0 * jnp.arange(1, MOBA_Q_HEADS + 1, dtype=F32) / MOBA_Q_HEADS)
    qkv_p = matmul(hp, w_qkv, tm=1024, tn=512, name="moba_qkv_p")
    qn_p, kn_p, km = moba_qknorm(qkv_p, g_q, g_k)
    nqt = seq // MOBA_BLOCK
    kmean = km.reshape(nbatch, nqt, MOBA_KV_HEADS, HEAD_DIM).transpose(0, 2, 1, 3)
    o_p = moba_prompt_attention(qn_p, kn_p, qkv_p, kmean, slopes, nbatch=nbatch, seq=seq)
    xp = matmul(o_p, w_o, xp, tm=1024, tn=512, name="moba_wo_p")
    qkv_s = matmul(hs, w_qkv, tm=hs.shape[0], tn=512, name="moba_qkv_s")
    qn_s, kn_s, _ = moba_qknorm(qkv_s, g_q, g_k)
    vn_s = qkv_s[:, nq + nk:]
    r = t * MOBA_GROUP
    q5 = qn_s.reshape(t, ns, MOBA_KV_HEADS, MOBA_GROUP, HEAD_DIM).transpose(1, 2, 0, 3, 4)
    q5 = q5.reshape(ns, MOBA_KV_HEADS, r, HEAD_DIM)
    zeros = jnp.zeros_like(q5[:, 0])
    q_bd = jnp.concatenate([jnp.concatenate([q5[:, 0], zeros], axis=-1),
                            jnp.concatenate([zeros, q5[:, 1]], axis=-1)], axis=1)
    rows = np.arange(MOBA_KV_HEADS * r)
    row_kvh, row_t, row_g = rows // r, (rows % r) // MOBA_GROUP, rows % MOBA_GROUP
    slope_rows = slopes[row_kvh * MOBA_GROUP + row_g].reshape(-1, 1)
    tpos_rows = jnp.asarray((past_len + row_t).astype(np.float32).reshape(-1, 1))
    o_bd = moba_sample_attention(page_table, cache_k, cache_v, q_bd, _pad_rows8(kn_s, ns), _pad_rows8(vn_s, ns),
                                 slope_rows, tpos_rows, past_len=past_len)
    o5 = jnp.stack([o_bd[:, :r, :HEAD_DIM], o_bd[:, r:, HEAD_DIM:]], axis=1)
    o_s = o5.reshape(ns, MOBA_KV_HEADS, t, MOBA_GROUP, HEAD_DIM).transpose(2, 0, 1, 3, 4)
    o_s = o_s.reshape(t * ns, nq).astype(BF16)
    xs = matmul(o_s, w_o, xs, tm=o_s.shape[0], tn=512, name="moba_wo_s")
    return xp, xs, kn_p, qkv_p[:, nq + nk:], kn_s, vn_s


def _mla_layer(hp, hs, xp, xs, cache_ckv, cache_kpe, page_table, w_down, g_cq, w_uq, g_q, g_ckv, g_kpe,
               w_uk, w_uv, w_o, *, nbatch, seq, ns, t, past_len):
    h = MLA_HEADS
    c1 = MLA_Q_LORA + MLA_KV_LORA
    w_kpe = w_down[:, c1:]
    zpad = jnp.zeros((w_down.shape[0], 128 - MLA_ROPE), w_down.dtype)
    w_down_x = jnp.concatenate([w_down[:, :c1], w_kpe, zpad, _swap_halves(w_kpe), zpad], axis=1)
    w_uq3 = w_uq.reshape(MLA_Q_LORA, h, MLA_QK_DIM)
    w_rope = w_uq3[:, :, MLA_NOPE:]
    w_uq_x = jnp.concatenate([w_uq3[:, :, :MLA_NOPE].reshape(MLA_Q_LORA, -1),
                              w_rope.reshape(MLA_Q_LORA, -1),
                              _swap_halves(w_rope).reshape(MLA_Q_LORA, -1)], axis=1)
    w_kv = jnp.concatenate([w_uk.reshape(MLA_KV_LORA, -1), w_uv.reshape(MLA_KV_LORA, -1)], axis=1)
    g_n, g_r = g_q[:MLA_NOPE], g_q[MLA_NOPE:]
    g_r2, g_rs2 = jnp.tile(g_r, 2), jnp.tile(_swap_halves(g_r), 2)
    pos_p = jnp.arange(seq, dtype=jnp.int32)
    pos_s = jnp.repeat(past_len + jnp.arange(t, dtype=jnp.int32), ns)

    def project(hx, pos, tm):
        cos2, sin2 = _rope_tables(pos, 1)
        cos4, sin4 = _rope_tables(pos, 2)
        d = matmul(hx, w_down_x, tm=tm, tn=256, name="mla_down")
        cq, ckv, ckv_b, kpe = mla_kv_prep(d, g_cq, g_ckv, g_kpe, _swap_halves(g_kpe), cos2, sin2)
        qx = matmul(cq, w_uq_x, tm=tm, tn=512, name="mla_uq")
        qn, qpe = mla_q_prep(qx, g_n, g_r2, g_rs2, cos4, sin4)
        return qn, qpe, ckv, ckv_b, kpe

    qn_p, qpe_p, ckv_p, ckvb_p, kpe_p = project(hp, pos_p, 1024)
    kv_p = matmul(ckvb_p, w_kv, tm=1024, tn=512, out_dtype=BF16, name="mla_kv_up")
    o_p = mla_prompt_attention(qn_p, qpe_p, kv_p, kpe_p, nbatch=nbatch, seq=seq)
    xp = matmul(o_p, w_o, xp, tm=1024, tn=512, name="mla_wo_p")
    m_s = hs.shape[0]
    qn_s, qpe_s, ckv_s, _, kpe_s = project(hs, pos_s, m_s)
    w_uk_t = w_uk.transpose(1, 2, 0)
    q_lat = head_matmul(qn_s, w_uk_t, out_dtype=BF16, name="mla_q_lat")
    r = t * h
    q_lat = q_lat.reshape(t, ns, h, MLA_KV_LORA).transpose(1, 0, 2, 3).reshape(ns, r, MLA_KV_LORA)
    q_pe = qpe_s.reshape(t, ns, h, MLA_ROPE).transpose(1, 0, 2, 3).reshape(ns, r, MLA_ROPE)
    tpos_rows = jnp.asarray((past_len + np.arange(r) // h).astype(np.float32).reshape(-1, 1))
    o_lat = mla_sample_attention(page_table, cache_ckv, cache_kpe, q_lat, q_pe,
                                 _pad_rows8(ckv_s, ns), _pad_rows8(kpe_s, ns), tpos_rows, past_len=past_len)
    o_lat = o_lat.reshape(ns, t, h, MLA_KV_LORA).transpose(1, 0, 2, 3).reshape(t * ns, h * MLA_KV_LORA)
    o_s = head_matmul(o_lat, w_uv.transpose(1, 0, 2), out_dtype=BF16, name="mla_o_up")
    xs = matmul(o_s, w_o, xs, tm=m_s, tn=512, name="mla_wo_s")
    return xp, xs, ckv_p, kpe_p, ckv_s, kpe_s


def _conv_ffn(xp, xs, g_norm, state_s, w_gate, w_up, conv_w, conv_b, w_down, *, nbatch, seq, ns):
    hp = rmsnorm_rows(xp, g_norm)
    hs = rmsnorm_rows(xs, g_norm)
    w_down_b = w_down.astype(BF16)
    zero_state = jnp.zeros((nbatch, CONV_W - 1, w_gate.shape[1]), F32)
    act_p, st_p = ffn_up_prompt(hp, w_gate, w_up, conv_w, conv_b, zero_state, seq=seq)
    xp = matmul(act_p, w_down_b, xp, tm=512, tn=512, name="ffn_down_p")
    act_s, st_s = ffn_up_sample(hs, w_gate, w_up, conv_w, conv_b, state_s.transpose(1, 0, 2), nseq=ns)
    xs = matmul(act_s, w_down_b, xs, tm=xs.shape[0], tn=512, name="ffn_down_s")
    return xp, xs, st_p, st_s.transpose(1, 0, 2)


def kernel(x_prompt, x_sample, cache_moba_k, cache_moba_v, cache_mla_ckv, cache_mla_kpe, state_ffn_conv, page_table, g_mix_norm, g_ffn_norm, moba_w_qkv, moba_g_q, moba_g_k, moba_w_o, mla_w_down, mla_g_cq, mla_w_uq, mla_g_q, mla_g_ckv, mla_g_kpe, mla_w_uk, mla_w_uv, mla_w_o, ffn_w_gate, ffn_w_up, ffn_conv_w, ffn_conv_b, ffn_w_down):
    nbatch, seq, d = x_prompt.shape
    ns, t, _ = x_sample.shape
    n_pool = cache_moba_k.shape[1]
    past_len = page_table.shape[1] * PAGE_SIZE
    depth = g_mix_norm.shape[0]
    dims = dict(nbatch=nbatch, seq=seq, ns=ns, t=t, past_len=past_len)
    xp = x_prompt.reshape(nbatch * seq, d)
    xs = x_sample.transpose(1, 0, 2).reshape(t * ns, d)
    outs = {k: [] for k in ("mk_p", "mv_p", "mk_s", "mv_s", "mc_p", "mr_p", "mc_s", "mr_s", "cv_p", "cv_s")}

    def seq_major(x):
        return x.reshape(t, ns, -1).transpose(1, 0, 2)

    for i in range(depth):
        hp = rmsnorm_rows(xp, g_mix_norm[i])
        hs = rmsnorm_rows(xs, g_mix_norm[i])
        j = i // 2
        if i % 2 == 0:
            ck = cache_moba_k[j].reshape(n_pool, PAGE_SIZE * MOBA_KV_HEADS, HEAD_DIM)
            cv = cache_moba_v[j].reshape(n_pool, PAGE_SIZE * MOBA_KV_HEADS, HEAD_DIM)
            xp, xs, kp, vp, ks, vs = _moba_layer(hp, hs, xp, xs, ck, cv, page_table, moba_w_qkv[j], moba_g_q[j],
                                                 moba_g_k[j], moba_w_o[j], **dims)
            outs["mk_p"].append(kp.reshape(nbatch, seq, MOBA_KV_HEADS, HEAD_DIM))
            outs["mv_p"].append(vp.reshape(nbatch, seq, MOBA_KV_HEADS, HEAD_DIM))
            outs["mk_s"].append(seq_major(ks).reshape(ns, t, MOBA_KV_HEADS, HEAD_DIM))
            outs["mv_s"].append(seq_major(vs).reshape(ns, t, MOBA_KV_HEADS, HEAD_DIM))
        else:
            xp, xs, cp, rp, cs, rs = _mla_layer(hp, hs, xp, xs, cache_mla_ckv[j],
                                                jnp.swapaxes(cache_mla_kpe[j], 1, 2), page_table,
                                                mla_w_down[j], mla_g_cq[j], mla_w_uq[j], mla_g_q[j], mla_g_ckv[j],
                                                mla_g_kpe[j], mla_w_uk[j], mla_w_uv[j], mla_w_o[j], **dims)
            outs["mc_p"].append(cp.reshape(nbatch, seq, -1))
            outs["mr_p"].append(rp.reshape(nbatch, seq, -1))
            outs["mc_s"].append(seq_major(cs))
            outs["mr_s"].append(seq_major(rs))
        xp, xs, st_p, st_s = _conv_ffn(xp, xs, g_ffn_norm[i], state_ffn_conv[i], ffn_w_gate[i], ffn_w_up[i],
                                       ffn_conv_w[i], ffn_conv_b[i], ffn_w_down[i], nbatch=nbatch, seq=seq, ns=ns)
        outs["cv_p"].append(st_p)
        outs["cv_s"].append(st_s)
    y_p = xp.reshape(nbatch, seq, d)
    y_s = seq_major(xs)
    return (y_p, y_s, jnp.stack(outs["mk_p"]), jnp.stack(outs["mv_p"]), jnp.stack(outs["mk_s"]),
            jnp.stack(outs["mv_s"]), jnp.stack(outs["mc_p"]), jnp.stack(outs["mr_p"]), jnp.stack(outs["mc_s"]),
            jnp.stack(outs["mr_s"]), jnp.stack(outs["cv_p"]), jnp.stack(outs["cv_s"]))
```

```python
import functools

import numpy as np
import jax
import jax.numpy as jnp
from jax import lax
from jax.experimental import pallas as pl
from jax.experimental.pallas import tpu as pltpu

F32 = jnp.float32
BF16 = jnp.bfloat16

D_MODEL = 4096
HEAD_DIM = 128
MOBA_Q_HEADS = 32
MOBA_KV_HEADS = 2
MOBA_GROUP = 16
MOBA_BLOCK = 256
MOBA_TOPK = 3
MLA_HEADS = 32
MLA_Q_LORA = 1024
MLA_KV_LORA = 512
MLA_NOPE = 128
MLA_ROPE = 64
MLA_QK_DIM = MLA_NOPE + MLA_ROPE
MLA_V_DIM = 128
ROPE_THETA = 10000.0
CONV_W = 3
NORM_EPS = 1e-6
PAGE_SIZE = 128

NEG = -1e30
VMEM_LIMIT = 56 * 1024 * 1024
NT = (((1,), (1,)), ((), ()))


def _params(sem):
    return pltpu.CompilerParams(dimension_semantics=sem, vmem_limit_bytes=VMEM_LIMIT)


def _rmsnorm_kernel(x_ref, g_ref, o_ref):
    x = x_ref[...]
    y = x * lax.rsqrt(jnp.mean(x * x, axis=-1, keepdims=True) + NORM_EPS)
    o_ref[...] = (y * g_ref[...]).astype(o_ref.dtype)


def rmsnorm_rows(x, g, tr=256):
    m, d = x.shape
    return pl.pallas_call(
        _rmsnorm_kernel,
        grid=(m // tr,),
        in_specs=[pl.BlockSpec((tr, d), lambda i: (i, 0)),
                  pl.BlockSpec((1, d), lambda i: (0, 0))],
        out_specs=pl.BlockSpec((tr, d), lambda i: (i, 0)),
        out_shape=jax.ShapeDtypeStruct((m, d), BF16),
        compiler_params=_params(("parallel",)),
        name="rmsnorm",
    )(x, g.reshape(1, d))


def _mm_kernel(a_ref, w_ref, *rest, has_res):
    o_ref = rest[-1]
    acc = jnp.dot(a_ref[...].astype(BF16), w_ref[...].astype(BF16), preferred_element_type=F32)
    if has_res:
        acc = rest[0][...] + acc
    o_ref[...] = acc.astype(o_ref.dtype)


def matmul(a, w, res=None, *, tm, tn, out_dtype=F32, name="matmul"):
    m, k = a.shape
    n = w.shape[1]
    assert m % tm == 0 and n % tn == 0, (m, n, tm, tn)
    in_specs = [pl.BlockSpec((tm, k), lambda i, j: (i, 0)),
                pl.BlockSpec((k, tn), lambda i, j: (0, j))]
    args = [a, w]
    if res is not None:
        in_specs.append(pl.BlockSpec((tm, tn), lambda i, j: (i, j)))
        args.append(res)
    return pl.pallas_call(
        functools.partial(_mm_kernel, has_res=res is not None),
        grid=(m // tm, n // tn),
        in_specs=in_specs,
        out_specs=pl.BlockSpec((tm, tn), lambda i, j: (i, j)),
        out_shape=jax.ShapeDtypeStruct((m, n), out_dtype),
        compiler_params=_params(("parallel", "parallel")),
        name=name,
    )(*args)


def _hmm_kernel(a_ref, w_ref, o_ref):
    o_ref[...] = jnp.dot(a_ref[...].astype(BF16), w_ref[...].astype(BF16),
                         preferred_element_type=F32).astype(o_ref.dtype)


def head_matmul(a, w, *, out_dtype, name):
    m = a.shape[0]
    nh, ka, n = w.shape
    return pl.pallas_call(
        _hmm_kernel,
        grid=(nh,),
        in_specs=[pl.BlockSpec((m, ka), lambda h: (0, h)),
                  pl.BlockSpec((None, ka, n), lambda h: (h, 0, 0))],
        out_specs=pl.BlockSpec((m, n), lambda h: (0, h)),
        out_shape=jax.ShapeDtypeStruct((m, nh * n), out_dtype),
        compiler_params=_params(("parallel",)),
        name=name,
    )(a, w)


def _silu_mul(gc, u):
    return (gc * (1.0 / (1.0 + jnp.exp(-gc)))) * u


def _ffn_up_prompt_kernel(a_ref, wg_ref, wu_ref, cw_ref, cb_ref, st_ref, act_ref, ns_ref):
    a = a_ref[...]
    g = jnp.dot(a, wg_ref[...].astype(BF16), preferred_element_type=F32)
    u = jnp.dot(a, wu_ref[...].astype(BF16), preferred_element_type=F32)
    tm = g.shape[0]
    row = lax.broadcasted_iota(jnp.int32, g.shape, 0)
    st0 = st_ref[0, 0:1, :]
    st1 = st_ref[0, 1:2, :]
    g1 = jnp.where(row >= 1, pltpu.roll(g, 1, 0), st1)
    g2 = jnp.where(row >= 2, pltpu.roll(g, 2, 0), jnp.where(row == 1, st1, st0))
    cw = cw_ref[...]
    gc = cb_ref[...] + cw[0:1, :] * g2
    gc = gc + cw[1:2, :] * g1
    gc = gc + cw[2:3, :] * g
    act_ref[...] = _silu_mul(gc, u).astype(act_ref.dtype)
    ns_ref[0] = g[tm - 2:tm, :]


def ffn_up_prompt(h, wg, wu, cw, cb, state, *, layer, seq, tn=256):
    m, k = h.shape
    n = wg.shape[2]
    nb = m // seq
    return pl.pallas_call(
        _ffn_up_prompt_kernel,
        grid=(nb, n // tn),
        in_specs=[pl.BlockSpec((seq, k), lambda i, j: (i, 0), pipeline_mode=pl.Buffered(1)),
                  pl.BlockSpec((None, k, tn), lambda i, j: (layer, 0, j)),
                  pl.BlockSpec((None, k, tn), lambda i, j: (layer, 0, j)),
                  pl.BlockSpec((CONV_W, tn), lambda i, j: (0, j)),
                  pl.BlockSpec((1, tn), lambda i, j: (0, j)),
                  pl.BlockSpec((1, 2, tn), lambda i, j: (i, 0, j))],
        out_specs=[pl.BlockSpec((seq, tn), lambda i, j: (i, j)),
                   pl.BlockSpec((1, 2, tn), lambda i, j: (i, 0, j))],
        out_shape=[jax.ShapeDtypeStruct((m, n), BF16),
                   jax.ShapeDtypeStruct((nb, 2, n), F32)],
        compiler_params=_params(("parallel", "parallel")),
        name="ffn_up_prompt",
    )(h, wg, wu, cw, cb.reshape(1, n), state)


def _ffn_up_sample_kernel(a_ref, wg_ref, wu_ref, cw_ref, cb_ref, st_ref, act_ref, ns_ref, *, nseq):
    a = a_ref[...]
    g = jnp.dot(a, wg_ref[...].astype(BF16), preferred_element_type=F32)
    u = jnp.dot(a, wu_ref[...].astype(BF16), preferred_element_type=F32)
    tm = g.shape[0]
    st0 = st_ref[0]
    st1 = st_ref[1]
    g1 = jnp.concatenate([st1, g[:tm - nseq]], axis=0)
    g2 = jnp.concatenate([st0, st1, g[:tm - 2 * nseq]], axis=0)
    cw = cw_ref[...]
    gc = cb_ref[...] + cw[0:1, :] * g2
    gc = gc + cw[1:2, :] * g1
    gc = gc + cw[2:3, :] * g
    act_ref[...] = _silu_mul(gc, u).astype(act_ref.dtype)
    ns_ref[0] = g[tm - 2 * nseq:tm - nseq]
    ns_ref[1] = g[tm - nseq:]


def ffn_up_sample(h, wg, wu, cw, cb, state_t, *, layer, nseq, tn=256):
    m, k = h.shape
    n = wg.shape[2]
    return pl.pallas_call(
        functools.partial(_ffn_up_sample_kernel, nseq=nseq),
        grid=(n // tn,),
        in_specs=[pl.BlockSpec((m, k), lambda j: (0, 0)),
                  pl.BlockSpec((None, k, tn), lambda j: (layer, 0, j)),
                  pl.BlockSpec((None, k, tn), lambda j: (layer, 0, j)),
                  pl.BlockSpec((CONV_W, tn), lambda j: (0, j)),
                  pl.BlockSpec((1, tn), lambda j: (0, j)),
                  pl.BlockSpec((2, nseq, tn), lambda j: (0, 0, j))],
        out_specs=[pl.BlockSpec((m, tn), lambda j: (0, j)),
                   pl.BlockSpec((2, nseq, tn), lambda j: (0, 0, j))],
        out_shape=[jax.ShapeDtypeStruct((m, n), BF16),
                   jax.ShapeDtypeStruct((2, nseq, n), F32)],
        compiler_params=_params(("parallel",)),
        name="ffn_up_sample",
    )(h, wg, wu, cw, cb.reshape(1, n), state_t)


def _moba_qknorm_kernel(x_ref, gq_ref, gk_ref, q_ref, k_ref, ka_ref, km_ref, *, nblk):
    gq = gq_ref[...]
    gk = gk_ref[...]
    for h in range(MOBA_Q_HEADS):
        x = x_ref[:, h * HEAD_DIM:(h + 1) * HEAD_DIM]
        y = x * lax.rsqrt(jnp.mean(x * x, axis=-1, keepdims=True) + NORM_EPS)
        q_ref[:, h * HEAD_DIM:(h + 1) * HEAD_DIM] = y * gq
    lane = lax.broadcasted_iota(jnp.int32, (x_ref.shape[0], HEAD_DIM), 1)
    onehot = jnp.where(lane == pl.program_id(0) % nblk, 1.0, 0.0).astype(ka_ref.dtype)
    for j in range(MOBA_KV_HEADS):
        c0 = (MOBA_Q_HEADS + j) * HEAD_DIM
        x = x_ref[:, c0:c0 + HEAD_DIM]
        y = (x * lax.rsqrt(jnp.mean(x * x, axis=-1, keepdims=True) + NORM_EPS)) * gk
        k_ref[:, j * HEAD_DIM:(j + 1) * HEAD_DIM] = y
        ka_ref[:, 2 * j * HEAD_DIM:(2 * j + 1) * HEAD_DIM] = y.astype(ka_ref.dtype)
        ka_ref[:, (2 * j + 1) * HEAD_DIM:(2 * j + 2) * HEAD_DIM] = onehot
        km_ref[0, :, j * HEAD_DIM:(j + 1) * HEAD_DIM] = jnp.mean(y, axis=0, keepdims=True)


def moba_qknorm(qkv, gq, gk, *, nblk):
    m, n = qkv.shape
    tr = MOBA_BLOCK
    nq = MOBA_Q_HEADS * HEAD_DIM
    nk = MOBA_KV_HEADS * HEAD_DIM
    return pl.pallas_call(
        functools.partial(_moba_qknorm_kernel, nblk=nblk),
        grid=(m // tr,),
        in_specs=[pl.BlockSpec((tr, n), lambda i: (i, 0)),
                  pl.BlockSpec((1, HEAD_DIM), lambda i: (0, 0)),
                  pl.BlockSpec((1, HEAD_DIM), lambda i: (0, 0))],
        out_specs=[pl.BlockSpec((tr, nq), lambda i: (i, 0)),
                   pl.BlockSpec((tr, nk), lambda i: (i, 0)),
                   pl.BlockSpec((tr, 2 * nk), lambda i: (i, 0)),
                   pl.BlockSpec((1, 1, nk), lambda i: (i, 0, 0))],
        out_shape=[jax.ShapeDtypeStruct((m, nq), F32),
                   jax.ShapeDtypeStruct((m, nk), F32),
                   jax.ShapeDtypeStruct((m, 2 * nk), BF16),
                   jax.ShapeDtypeStruct((m // tr, 1, nk), F32)],
        compiler_params=_params(("parallel",)),
        name="moba_qknorm",
    )(qkv, gq.reshape(1, HEAD_DIM), gk.reshape(1, HEAD_DIM))


def _top3(gate, allowed, axis=1):
    nb = gate.shape[axis]
    pos = lax.broadcasted_iota(jnp.int32, gate.shape, axis)
    gm = jnp.where(allowed, gate, -jnp.inf)
    sel = jnp.zeros(gate.shape, F32)
    for _ in range(MOBA_TOPK):
        mx = jnp.max(gm, axis=axis, keepdims=True)
        idx = jnp.min(jnp.where(gm == mx, pos, nb), axis=axis, keepdims=True)
        pick = pos == idx
        sel = jnp.where(pick, 1.0, sel)
        gm = jnp.where(pick, -jnp.inf, gm)
    return jnp.where(allowed, sel, 0.0)


def _online_update(carry, s, ok, v):
    m, l, acc = carry
    s = jnp.where(ok, s, NEG)
    m_new = jnp.maximum(m, jnp.max(s, axis=1, keepdims=True))
    p = jnp.where(ok, jnp.exp(s - m_new), 0.0)
    alpha = jnp.exp(m - m_new)
    l = alpha * l + jnp.sum(p, axis=1, keepdims=True)
    acc = alpha * acc + jnp.dot(p.astype(BF16), v, preferred_element_type=F32)
    return m_new, l, acc


def _moba_prompt_kernel(slopes_ref, q_ref, ka_ref, vt_ref, km_ref, o_ref, *, hstack, nbp):
    kvh = pl.program_id(1)
    qt = pl.program_id(2)
    blk = MOBA_BLOCK
    cols = hstack * blk
    scale = HEAD_DIM ** -0.5
    km = km_ref[0, 0]
    bidx = lax.broadcasted_iota(jnp.int32, (nbp, cols), 0)
    cr = (lax.broadcasted_iota(jnp.int32, (blk, cols), 0)
          - lax.broadcasted_iota(jnp.int32, (blk, cols), 1) % blk)
    crf = cr.astype(F32)
    off_own = pl.multiple_of(qt * blk, blk)
    ka_own = ka_ref[pl.ds(off_own, blk), :]
    vt_own = vt_ref[0, 0, qt]
    pad = jnp.zeros((HEAD_DIM - nbp, cols), F32)
    for hg in range(MOBA_GROUP // hstack):
        heads = [hg * hstack + i for i in range(hstack)]
        qs = jnp.concatenate([q_ref[:, h * HEAD_DIM:(h + 1) * HEAD_DIM] for h in heads], axis=0)
        slope = jnp.concatenate([jnp.full((1, blk), slopes_ref[kvh * MOBA_GROUP + h], F32) for h in heads], axis=1)
        gate = lax.dot_general(km, qs, NT, precision=lax.Precision.HIGHEST, preferred_element_type=F32)
        sel = _top3(gate, bidx < qt, axis=0)
        blockbias = jnp.where((sel > 0.5) | (bidx == qt), 0.0, NEG)
        qa = jnp.concatenate([(qs * scale).T, blockbias, pad], axis=0).astype(BF16)
        base = slope * crf
        s = jnp.dot(ka_own, qa, preferred_element_type=F32) + base
        s = jnp.where(cr <= 0, s, NEG)
        m = jnp.max(s, axis=0, keepdims=True)
        p = jnp.exp(s - m)
        l = jnp.sum(p, axis=0, keepdims=True)
        acc = jnp.dot(vt_own, p.astype(BF16), preferred_element_type=F32)

        def body(n, carry, qa=qa, base=base, slope=slope):
            m, l, acc = carry
            off = pl.multiple_of(n * blk, blk)
            s = jnp.dot(ka_ref[pl.ds(off, blk), :], qa, preferred_element_type=F32) + base
            qterm = slope * (jnp.zeros((1, cols), jnp.int32) + (n - qt) * blk).astype(F32)
            m_new = jnp.maximum(m, jnp.max(s, axis=0, keepdims=True) + qterm)
            p = jnp.exp(s - (m_new - qterm))
            alpha = jnp.exp(m - m_new)
            l = alpha * l + jnp.sum(p, axis=0, keepdims=True)
            acc = alpha * acc + jnp.dot(vt_ref[0, 0, n], p.astype(BF16), preferred_element_type=F32)
            return m_new, l, acc

        m, l, acc = lax.fori_loop(0, qt, body, (m, l, acc))
        o = (acc / l).T
        for i, h in enumerate(heads):
            o_ref[:, h * HEAD_DIM:(h + 1) * HEAD_DIM] = o[i * blk:(i + 1) * blk].astype(o_ref.dtype)


def moba_prompt_attention(qn, ka, vt, kmean, slopes, *, nbatch, seq, hstack=4):
    m = qn.shape[0]
    nqt = seq // MOBA_BLOCK
    nbp = kmean.shape[2]
    assert nqt <= nbp <= HEAD_DIM and nbp % 8 == 0
    gw = MOBA_GROUP * HEAD_DIM
    return pl.pallas_call(
        functools.partial(_moba_prompt_kernel, hstack=hstack, nbp=nbp),
        grid=(nbatch, MOBA_KV_HEADS, nqt),
        in_specs=[pl.BlockSpec(memory_space=pltpu.SMEM),
                  pl.BlockSpec((MOBA_BLOCK, gw), lambda b, k, t: (b * nqt + t, k)),
                  pl.BlockSpec((seq, 2 * HEAD_DIM), lambda b, k, t: (b, k)),
                  pl.BlockSpec((1, 1, nqt, HEAD_DIM, MOBA_BLOCK), lambda b, k, t: (b, k, 0, 0, 0)),
                  pl.BlockSpec((1, 1, nbp, HEAD_DIM), lambda b, k, t: (b, k, 0, 0))],
        out_specs=pl.BlockSpec((MOBA_BLOCK, gw), lambda b, k, t: (b * nqt + t, k)),
        out_shape=jax.ShapeDtypeStruct((m, MOBA_Q_HEADS * HEAD_DIM), BF16),
        compiler_params=_params(("parallel", "parallel", "parallel")),
        name="moba_prompt_attn",
    )(slopes, qn, ka, vt, kmean)


def _kv_chunk(page_refs):
    pages = [jnp.concatenate([r[pl.ds(j, PAGE_SIZE, stride=MOBA_KV_HEADS), :] for j in range(MOBA_KV_HEADS)], axis=1)
             for r in page_refs]
    return jnp.concatenate(pages, axis=0)


def _moba_sample_kernel(pt_ref, *refs, pps, nchunk, past_len):
    del pt_ref
    k_refs = refs[:pps]
    v_refs = refs[pps:2 * pps]
    q_ref, kn_ref, vn_ref, slope_ref, tpos_ref, o_ref = refs[2 * pps:2 * pps + 6]
    s_ref, km_ref, sel_ref, m_ref, l_ref, acc_ref = refs[2 * pps + 6:]
    c = pl.program_id(1)
    scale = HEAD_DIM ** -0.5
    ck = pps * PAGE_SIZE
    bpc = ck // MOBA_BLOCK
    nblk = nchunk * bpc
    slope = slope_ref[...]
    tpos = tpos_ref[...]

    @pl.when(c < nchunk)
    def _scores():
        kc = _kv_chunk(k_refs)
        s = lax.dot_general(q_ref[0].astype(BF16), kc.astype(BF16), NT, preferred_element_type=F32) * scale
        kpos = (c * ck + lax.broadcasted_iota(jnp.int32, s.shape, 1)).astype(F32)
        s_ref[c] = s - slope * (tpos - kpos)
        km_ref[c] = jnp.sum(kc.reshape(bpc, MOBA_BLOCK, kc.shape[1]), axis=1) * (1.0 / MOBA_BLOCK)

    @pl.when(c == nchunk)
    def _select():
        km = km_ref[...].reshape(nblk, km_ref.shape[2])
        gate = lax.dot_general(q_ref[0], km, NT, precision=lax.Precision.HIGHEST,
                               preferred_element_type=F32)
        sel_ref[...] = _top3(gate, jnp.full(gate.shape, True))
        m_ref[...] = jnp.full(m_ref.shape, NEG, F32)
        l_ref[...] = jnp.zeros(l_ref.shape, F32)
        acc_ref[...] = jnp.zeros(acc_ref.shape, F32)

    @pl.when(c >= nchunk)
    def _attend():
        cc = c - nchunk
        s = s_ref[cc]
        brow = lax.broadcasted_iota(jnp.int32, (nblk, ck), 0)
        bcol = cc * bpc + lax.broadcasted_iota(jnp.int32, (nblk, ck), 1) // MOBA_BLOCK
        expand = jnp.where(brow == bcol, 1.0, 0.0).astype(BF16)
        ok = jnp.dot(sel_ref[...].astype(BF16), expand, preferred_element_type=F32) > 0.5
        vc = _kv_chunk(v_refs).astype(BF16)
        m, l, acc = _online_update((m_ref[...], l_ref[...], acc_ref[...]), s, ok, vc)
        m_ref[...] = m
        l_ref[...] = l
        acc_ref[...] = acc

    @pl.when(c == 2 * nchunk - 1)
    def _own_block():
        kn = kn_ref[0]
        s = lax.dot_general(q_ref[0].astype(BF16), kn.astype(BF16), NT, preferred_element_type=F32) * scale
        j = lax.broadcasted_iota(jnp.int32, s.shape, 1).astype(F32)
        dist = tpos - (past_len + j)
        s = s - slope * dist
        m, l, acc = _online_update((m_ref[...], l_ref[...], acc_ref[...]), s, dist >= 0,
                                   vn_ref[0].astype(BF16))
        o_ref[0] = acc / l


def moba_sample_attention(page_table, cache_k, cache_v, q_bd, k_new, v_new, slope_rows, tpos_rows,
                          *, past_len, pps=16):
    ns, r, w = q_bd.shape
    assert past_len % MOBA_BLOCK == 0 and (pps * PAGE_SIZE) % MOBA_BLOCK == 0
    npages = past_len // PAGE_SIZE
    nchunk = npages // pps
    prows = PAGE_SIZE * MOBA_KV_HEADS
    kspecs = [pl.BlockSpec((None, prows, HEAD_DIM),
                           lambda s, c, pt, j=j: (pt[s * npages + jnp.minimum(c, nchunk - 1) * pps + j], 0, 0))
              for j in range(pps)]
    vspecs = [pl.BlockSpec((None, prows, HEAD_DIM),
                           lambda s, c, pt, j=j: (pt[s * npages + jnp.maximum(c - nchunk, 0) * pps + j], 0, 0))
              for j in range(pps)]
    per_seq = lambda s, c, pt: (s, 0, 0)
    const2 = lambda s, c, pt: (0, 0)
    nblk = past_len // MOBA_BLOCK
    grid_spec = pltpu.PrefetchScalarGridSpec(
        num_scalar_prefetch=1,
        grid=(ns, 2 * nchunk),
        in_specs=kspecs + vspecs + [
            pl.BlockSpec((1, r, w), per_seq),
            pl.BlockSpec((1, 8, w), per_seq),
            pl.BlockSpec((1, 8, w), per_seq),
            pl.BlockSpec((r, 1), const2),
            pl.BlockSpec((r, 1), const2)],
        out_specs=pl.BlockSpec((1, r, w), per_seq),
        scratch_shapes=[pltpu.VMEM((nchunk, r, pps * PAGE_SIZE), F32),
                        pltpu.VMEM((nchunk, pps * PAGE_SIZE // MOBA_BLOCK, w), F32),
                        pltpu.VMEM((r, nblk), F32),
                        pltpu.VMEM((r, 1), F32),
                        pltpu.VMEM((r, 1), F32),
                        pltpu.VMEM((r, w), F32)])
    return pl.pallas_call(
        functools.partial(_moba_sample_kernel, pps=pps, nchunk=nchunk, past_len=past_len),
        grid_spec=grid_spec,
        out_shape=jax.ShapeDtypeStruct((ns, r, w), F32),
        compiler_params=_params(("parallel", "arbitrary")),
        name="moba_sample_attn",
    )(page_table.reshape(-1), *([cache_k] * pps), *([cache_v] * pps),
      q_bd, k_new, v_new, slope_rows, tpos_rows)


def _mla_kv_prep_kernel(d_ref, gcq_ref, gckv_ref, gk_ref, gks_ref, cos_ref, sin_ref,
                        cq_ref, ckv_ref, ckvb_ref, kpe_ref):
    c0 = MLA_Q_LORA
    c1 = c0 + MLA_KV_LORA
    x = d_ref[:, 0:c0]
    y = x * lax.rsqrt(jnp.mean(x * x, axis=-1, keepdims=True) + NORM_EPS)
    cq_ref[...] = (y * gcq_ref[...]).astype(cq_ref.dtype)
    x = d_ref[:, c0:c1]
    y = (x * lax.rsqrt(jnp.mean(x * x, axis=-1, keepdims=True) + NORM_EPS)) * gckv_ref[...]
    ckv_ref[...] = y
    ckvb_ref[...] = y.astype(ckvb_ref.dtype)
    x = d_ref[:, c1:c1 + 128][:, :MLA_ROPE]
    xs = d_ref[:, c1 + 128:c1 + 256][:, :MLA_ROPE]
    rstd = lax.rsqrt(jnp.mean(x * x, axis=-1, keepdims=True) + NORM_EPS)
    kpe_ref[...] = ((x * rstd) * gk_ref[...]) * cos_ref[...] + ((xs * rstd) * gks_ref[...]) * sin_ref[...]


def mla_kv_prep(d, g_cq, g_ckv, g_kpe, g_kpe_sw, cos2, sin2s, *, tr=256):
    m, n = d.shape
    tab_rows = cos2.shape[0]
    ntab = tab_rows // tr if tab_rows >= tr else 1
    row = lambda i: (i, 0)
    const = lambda i: (0, 0)
    tab = lambda i: (i % ntab, 0)
    return pl.pallas_call(
        _mla_kv_prep_kernel,
        grid=(m // tr,),
        in_specs=[pl.BlockSpec((tr, n), row),
                  pl.BlockSpec((1, MLA_Q_LORA), const),
                  pl.BlockSpec((1, MLA_KV_LORA), const),
                  pl.BlockSpec((1, MLA_ROPE), const),
                  pl.BlockSpec((1, MLA_ROPE), const),
                  pl.BlockSpec((tr, MLA_ROPE), tab),
                  pl.BlockSpec((tr, MLA_ROPE), tab)],
        out_specs=[pl.BlockSpec((tr, MLA_Q_LORA), row),
                   pl.BlockSpec((tr, MLA_KV_LORA), row),
                   pl.BlockSpec((tr, MLA_KV_LORA), row),
                   pl.BlockSpec((tr, MLA_ROPE), row)],
        out_shape=[jax.ShapeDtypeStruct((m, MLA_Q_LORA), BF16),
                   jax.ShapeDtypeStruct((m, MLA_KV_LORA), F32),
                   jax.ShapeDtypeStruct((m, MLA_KV_LORA), BF16),
                   jax.ShapeDtypeStruct((m, MLA_ROPE), F32)],
        compiler_params=_params(("parallel",)),
        name="mla_kv_prep",
    )(d, g_cq.reshape(1, -1), g_ckv.reshape(1, -1), g_kpe.reshape(1, -1), g_kpe_sw.reshape(1, -1), cos2, sin2s)


def _mla_q_prep_kernel(x_ref, gn_ref, gr_ref, grs_ref, cos_ref, sin_ref, qn_ref, qpe_ref):
    scale = MLA_QK_DIM ** -0.5
    nn = MLA_HEADS * MLA_NOPE
    nr = MLA_HEADS * MLA_ROPE
    gn = gn_ref[...]
    gr = gr_ref[...]
    grs = grs_ref[...]
    cos = cos_ref[...]
    sin = sin_ref[...]
    lane = lax.broadcasted_iota(jnp.int32, (x_ref.shape[0], 128), 1)
    first = lane < MLA_ROPE
    for hp in range(MLA_HEADS // 2):
        r = x_ref[:, nn + hp * 128:nn + (hp + 1) * 128]
        rs = x_ref[:, nn + nr + hp * 128:nn + nr + (hp + 1) * 128]
        r2 = r * r
        ss_a = jnp.sum(jnp.where(first, r2, 0.0), axis=-1, keepdims=True)
        ss_b = jnp.sum(jnp.where(first, 0.0, r2), axis=-1, keepdims=True)
        rstd_pair = []
        for sub, ss_r in ((0, ss_a), (1, ss_b)):
            h = 2 * hp + sub
            n = x_ref[:, h * MLA_NOPE:(h + 1) * MLA_NOPE]
            ss = jnp.sum(n * n, axis=-1, keepdims=True) + ss_r
            rstd = lax.rsqrt(ss * (1.0 / MLA_QK_DIM) + NORM_EPS)
            qn_ref[:, h * MLA_NOPE:(h + 1) * MLA_NOPE] = (((n * rstd) * gn) * scale).astype(qn_ref.dtype)
            rstd_pair.append(rstd)
        rstd = jnp.where(first, rstd_pair[0], rstd_pair[1])
        qpe = ((r * rstd) * gr) * cos + ((rs * rstd) * grs) * sin
        qpe_ref[:, hp * 128:(hp + 1) * 128] = (qpe * scale).astype(qpe_ref.dtype)


def mla_q_prep(x, g_n, g_r2, g_rs2, cos4, sin4s, *, tr=256):
    m, n = x.shape
    tab_rows = cos4.shape[0]
    ntab = tab_rows // tr if tab_rows >= tr else 1
    row = lambda i: (i, 0)
    const = lambda i: (0, 0)
    tab = lambda i: (i % ntab, 0)
    return pl.pallas_call(
        _mla_q_prep_kernel,
        grid=(m // tr,),
        in_specs=[pl.BlockSpec((tr, n), row),
                  pl.BlockSpec((1, MLA_NOPE), const),
                  pl.BlockSpec((1, 128), const),
                  pl.BlockSpec((1, 128), const),
                  pl.BlockSpec((tr, 128), tab),
                  pl.BlockSpec((tr, 128), tab)],
        out_specs=[pl.BlockSpec((tr, MLA_HEADS * MLA_NOPE), row),
                   pl.BlockSpec((tr, MLA_HEADS * MLA_ROPE), row)],
        out_shape=[jax.ShapeDtypeStruct((m, MLA_HEADS * MLA_NOPE), BF16),
                   jax.ShapeDtypeStruct((m, MLA_HEADS * MLA_ROPE), BF16)],
        compiler_params=_params(("parallel",)),
        name="mla_q_prep",
    )(x, g_n.reshape(1, -1), g_r2.reshape(1, -1), g_rs2.reshape(1, -1), cos4, sin4s)


def _mla_prompt_kernel(qn_ref, qpe_ref, kn_ref, kpe_ref, v_ref, o_ref, *, tq):
    qt = pl.program_id(2)
    lane = lax.broadcasted_iota(jnp.int32, (tq, 128), 1)
    qpe = qpe_ref[...].astype(F32)
    halves = (lane < MLA_ROPE, lane >= MLA_ROPE)
    qs = [jnp.concatenate([qn_ref[:, i * MLA_NOPE:(i + 1) * MLA_NOPE],
                           jnp.where(halves[i], qpe, 0.0).astype(BF16)], axis=1) for i in range(2)]
    rc = (lax.broadcasted_iota(jnp.int32, (tq, tq), 0) - lax.broadcasted_iota(jnp.int32, (tq, tq), 1))

    def kv_block(n):
        off = pl.multiple_of(n * tq, tq)
        kpe = kpe_ref[pl.ds(off, tq), :]
        ks = [jnp.concatenate([kn_ref[pl.ds(off, tq), i * MLA_NOPE:(i + 1) * MLA_NOPE], kpe], axis=1)
              for i in range(2)]
        vs = [v_ref[pl.ds(off, tq), i * MLA_V_DIM:(i + 1) * MLA_V_DIM] for i in range(2)]
        return ks, vs

    ks, vs = kv_block(qt)
    carry = []
    for i in range(2):
        s = lax.dot_general(qs[i], ks[i], NT, preferred_element_type=F32)
        s = jnp.where(rc >= 0, s, NEG)
        m = jnp.max(s, axis=1, keepdims=True)
        p = jnp.exp(s - m)
        carry += [m, jnp.sum(p, axis=1, keepdims=True), jnp.dot(p.astype(BF16), vs[i], preferred_element_type=F32)]

    def body(n, carry):
        ks, vs = kv_block(n)
        out = []
        for i in range(2):
            m, l, acc = carry[3 * i:3 * i + 3]
            s = lax.dot_general(qs[i], ks[i], NT, preferred_element_type=F32)
            m_new = jnp.maximum(m, jnp.max(s, axis=1, keepdims=True))
            p = jnp.exp(s - m_new)
            alpha = jnp.exp(m - m_new)
            out += [m_new, alpha * l + jnp.sum(p, axis=1, keepdims=True),
                    alpha * acc + jnp.dot(p.astype(BF16), vs[i], preferred_element_type=F32)]
        return tuple(out)

    carry = lax.fori_loop(0, qt, body, tuple(carry))
    for i in range(2):
        m, l, acc = carry[3 * i:3 * i + 3]
        o_ref[:, i * MLA_V_DIM:(i + 1) * MLA_V_DIM] = (acc / l).astype(o_ref.dtype)


def mla_prompt_attention(qn, qpe, kv, kpe, *, nbatch, seq, tq=512):
    m = qn.shape[0]
    nqt = seq // tq
    npair = MLA_HEADS // 2
    kpe = jnp.concatenate([kpe, kpe], axis=1).astype(BF16)
    return pl.pallas_call(
        functools.partial(_mla_prompt_kernel, tq=tq),
        grid=(nbatch, npair, nqt),
        in_specs=[pl.BlockSpec((tq, 2 * MLA_NOPE), lambda b, h, t: (b * nqt + t, h)),
                  pl.BlockSpec((tq, 2 * MLA_ROPE), lambda b, h, t: (b * nqt + t, h)),
                  pl.BlockSpec((seq, 2 * MLA_NOPE), lambda b, h, t: (b, h)),
                  pl.BlockSpec((seq, 2 * MLA_ROPE), lambda b, h, t: (b, 0)),
                  pl.BlockSpec((seq, 2 * MLA_V_DIM), lambda b, h, t: (b, npair + h))],
        out_specs=pl.BlockSpec((tq, 2 * MLA_V_DIM), lambda b, h, t: (b * nqt + t, h)),
        out_shape=jax.ShapeDtypeStruct((m, MLA_HEADS * MLA_V_DIM), BF16),
        compiler_params=_params(("parallel", "parallel", "parallel")),
        name="mla_prompt_attn",
    )(qn, qpe, kv, kpe, kv)


def _mla_sample_kernel(pt_ref, *refs, pps, nchunk, past_len):
    del pt_ref
    c_refs = refs[:pps]
    r_refs = refs[pps:2 * pps]
    ql_ref, qp_ref, cn_ref, rn_ref, tpos_ref, o_ref, m_ref, l_ref, acc_ref = refs[2 * pps:]
    c = pl.program_id(1)
    ql = ql_ref[0]
    qp = qp_ref[0]

    @pl.when(c == 0)
    def _init():
        m_ref[...] = jnp.full(m_ref.shape, NEG, F32)
        l_ref[...] = jnp.zeros(l_ref.shape, F32)
        acc_ref[...] = jnp.zeros(acc_ref.shape, F32)

    cc = jnp.concatenate([r[...] for r in c_refs], axis=0).astype(BF16)
    rr = jnp.concatenate([r[...] for r in r_refs], axis=1).astype(BF16)
    s = (lax.dot_general(ql, cc, NT, preferred_element_type=F32)
         + jnp.dot(qp, rr, preferred_element_type=F32))
    m = m_ref[...]
    m_new = jnp.maximum(m, jnp.max(s, axis=1, keepdims=True))
    p = jnp.exp(s - m_new)
    alpha = jnp.exp(m - m_new)
    m_ref[...] = m_new
    l_ref[...] = alpha * l_ref[...] + jnp.sum(p, axis=1, keepdims=True)
    acc_ref[...] = alpha * acc_ref[...] + jnp.dot(p.astype(BF16), cc, preferred_element_type=F32)

    @pl.when(c == nchunk - 1)
    def _new_rows():
        cn = cn_ref[0].astype(BF16)
        rn = rn_ref[0].astype(BF16)
        s2 = (lax.dot_general(ql, cn, NT, preferred_element_type=F32)
              + lax.dot_general(qp, rn, NT, preferred_element_type=F32))
        j = lax.broadcasted_iota(jnp.int32, s2.shape, 1).astype(F32)
        ok = (past_len + j) <= tpos_ref[...]
        m2, l2, acc2 = _online_update((m_ref[...], l_ref[...], acc_ref[...]), s2, ok, cn)
        o_ref[0] = (acc2 / l2).astype(o_ref.dtype)


def mla_sample_attention(page_table, cache_ckv, cache_kpe, q_lat, q_pe, c_new, r_new, tpos_rows,
                         *, past_len, pps=16):
    ns, r, _ = q_lat.shape
    npages = past_len // PAGE_SIZE
    nchunk = npages // pps
    page = lambda s, c, pt, j=0: (pt[s * npages + c * pps + j], 0, 0)
    cspecs = [pl.BlockSpec((None, PAGE_SIZE, MLA_KV_LORA), functools.partial(page, j=j)) for j in range(pps)]
    rspecs = [pl.BlockSpec((None, MLA_ROPE, PAGE_SIZE), functools.partial(page, j=j)) for j in range(pps)]
    per_seq = lambda s, c, pt: (s, 0, 0)
    grid_spec = pltpu.PrefetchScalarGridSpec(
        num_scalar_prefetch=1,
        grid=(ns, nchunk),
        in_specs=cspecs + rspecs + [
            pl.BlockSpec((1, r, MLA_KV_LORA), per_seq),
            pl.BlockSpec((1, r, MLA_ROPE), per_seq),
            pl.BlockSpec((1, 8, MLA_KV_LORA), per_seq),
            pl.BlockSpec((1, 8, MLA_ROPE), per_seq),
            pl.BlockSpec((r, 1), lambda s, c, pt: (0, 0))],
        out_specs=pl.BlockSpec((1, r, MLA_KV_LORA), per_seq),
        scratch_shapes=[pltpu.VMEM((r, 1), F32), pltpu.VMEM((r, 1), F32), pltpu.VMEM((r, MLA_KV_LORA), F32)])
    return pl.pallas_call(
        functools.partial(_mla_sample_kernel, pps=pps, nchunk=nchunk, past_len=past_len),
        grid_spec=grid_spec,
        out_shape=jax.ShapeDtypeStruct((ns, r, MLA_KV_LORA), BF16),
        compiler_params=_params(("parallel", "arbitrary")),
        name="mla_sample_attn",
    )(page_table.reshape(-1), *([cache_ckv] * pps), *([cache_kpe] * pps),
      q_lat, q_pe, c_new, r_new, tpos_rows)


def _rope_tables(pos, reps):
    half = MLA_ROPE // 2
    inv = ROPE_THETA ** (-jnp.arange(half, dtype=F32) / half)
    ang = pos.astype(F32)[:, None] * inv[None, :]
    cos, sin = jnp.cos(ang), jnp.sin(ang)
    cos2 = jnp.concatenate([cos, cos], axis=1)
    sin2 = jnp.concatenate([-sin, sin], axis=1)
    return jnp.tile(cos2, (1, reps)), jnp.tile(sin2, (1, reps))


def _swap_halves(x):
    half = x.shape[-1] // 2
    return jnp.concatenate([x[..., half:], x[..., :half]], axis=-1)


def _pad_rows8(x, ns):
    t = x.shape[0] // ns
    y = x.reshape(t, ns, x.shape[1]).transpose(1, 0, 2)
    return jnp.pad(y, ((0, 0), (0, 8 - t), (0, 0)))


def _moba_layer(hp, hs, xp, xs, cache_k, cache_v, page_table, w_qkv, g_q, g_k, w_o, *, nbatch, seq, ns, t, past_len):
    nq = MOBA_Q_HEADS * HEAD_DIM
    nk = MOBA_KV_HEADS * HEAD_DIM
    slopes = jnp.exp2(-8.0 * jnp.arange(1, MOBA_Q_HEADS + 1, dtype=F32) / MOBA_Q_HEADS)
    qkv_p = matmul(hp, w_qkv, tm=1024, tn=512, name="moba_qkv_p")
    nqt = seq // MOBA_BLOCK
    qn_p, kn_p, ka_p, km = moba_qknorm(qkv_p, g_q, g_k, nblk=nqt)
    kmean = km.reshape(nbatch, nqt, MOBA_KV_HEADS, HEAD_DIM).transpose(0, 2, 1, 3)
    kmean = jnp.pad(kmean, ((0, 0), (0, 0), (0, -nqt % 8), (0, 0)))
    vt_p = qkv_p[:, nq + nk:].astype(BF16).reshape(nbatch, nqt, MOBA_BLOCK, MOBA_KV_HEADS, HEAD_DIM)
    vt_p = vt_p.transpose(0, 3, 1, 4, 2)
    o_p = moba_prompt_attention(qn_p, ka_p, vt_p, kmean, slopes, nbatch=nbatch, seq=seq)
    xp = matmul(o_p, w_o, xp, tm=1024, tn=512, name="moba_wo_p")
    qkv_s = matmul(hs, w_qkv, tm=hs.shape[0], tn=512, name="moba_qkv_s")
    qn_s, kn_s, _, _ = moba_qknorm(qkv_s, g_q, g_k, nblk=1)
    vn_s = qkv_s[:, nq + nk:]
    r = t * MOBA_GROUP
    q5 = qn_s.reshape(t, ns, MOBA_KV_HEADS, MOBA_GROUP, HEAD_DIM).transpose(1, 2, 0, 3, 4)
    q5 = q5.reshape(ns, MOBA_KV_HEADS, r, HEAD_DIM)
    zeros = jnp.zeros_like(q5[:, 0])
    q_bd = jnp.concatenate([jnp.concatenate([q5[:, 0], zeros], axis=-1),
                            jnp.concatenate([zeros, q5[:, 1]], axis=-1)], axis=1)
    rows = np.arange(MOBA_KV_HEADS * r)
    row_kvh, row_t, row_g = rows // r, (rows % r) // MOBA_GROUP, rows % MOBA_GROUP
    slope_rows = slopes[row_kvh * MOBA_GROUP + row_g].reshape(-1, 1)
    tpos_rows = jnp.asarray((past_len + row_t).astype(np.float32).reshape(-1, 1))
    o_bd = moba_sample_attention(page_table, cache_k, cache_v, q_bd, _pad_rows8(kn_s, ns), _pad_rows8(vn_s, ns),
                                 slope_rows, tpos_rows, past_len=past_len)
    o5 = jnp.stack([o_bd[:, :r, :HEAD_DIM], o_bd[:, r:, HEAD_DIM:]], axis=1)
    o_s = o5.reshape(ns, MOBA_KV_HEADS, t, MOBA_GROUP, HEAD_DIM).transpose(2, 0, 1, 3, 4)
    o_s = o_s.reshape(t * ns, nq).astype(BF16)
    xs = matmul(o_s, w_o, xs, tm=o_s.shape[0], tn=512, name="moba_wo_s")
    return xp, xs, kn_p, qkv_p[:, nq + nk:], kn_s, vn_s


def _mla_layer(hp, hs, xp, xs, cache_ckv, cache_kpe, page_table, w_down, g_cq, w_uq, g_q, g_ckv, g_kpe,
               w_uk, w_uv, w_o, *, nbatch, seq, ns, t, past_len):
    h = MLA_HEADS
    c1 = MLA_Q_LORA + MLA_KV_LORA
    w_kpe = w_down[:, c1:]
    zpad = jnp.zeros((w_down.shape[0], 128 - MLA_ROPE), w_down.dtype)
    w_down_x = jnp.concatenate([w_down[:, :c1], w_kpe, zpad, _swap_halves(w_kpe), zpad], axis=1)
    w_uq3 = w_uq.reshape(MLA_Q_LORA, h, MLA_QK_DIM)
    w_rope = w_uq3[:, :, MLA_NOPE:]
    w_uq_x = jnp.concatenate([w_uq3[:, :, :MLA_NOPE].reshape(MLA_Q_LORA, -1),
                              w_rope.reshape(MLA_Q_LORA, -1),
                              _swap_halves(w_rope).reshape(MLA_Q_LORA, -1)], axis=1)
    w_kv = jnp.concatenate([w_uk.reshape(MLA_KV_LORA, -1), w_uv.reshape(MLA_KV_LORA, -1)], axis=1)
    g_n, g_r = g_q[:MLA_NOPE], g_q[MLA_NOPE:]
    g_r2, g_rs2 = jnp.tile(g_r, 2), jnp.tile(_swap_halves(g_r), 2)
    pos_p = jnp.arange(seq, dtype=jnp.int32)
    pos_s = jnp.repeat(past_len + jnp.arange(t, dtype=jnp.int32), ns)

    def project(hx, pos, tm):
        cos2, sin2 = _rope_tables(pos, 1)
        cos4, sin4 = _rope_tables(pos, 2)
        d = matmul(hx, w_down_x, tm=tm, tn=256, name="mla_down")
        cq, ckv, ckv_b, kpe = mla_kv_prep(d, g_cq, g_ckv, g_kpe, _swap_halves(g_kpe), cos2, sin2)
        qx = matmul(cq, w_uq_x, tm=tm, tn=512, name="mla_uq")
        qn, qpe = mla_q_prep(qx, g_n, g_r2, g_rs2, cos4, sin4)
        return qn, qpe, ckv, ckv_b, kpe

    qn_p, qpe_p, ckv_p, ckvb_p, kpe_p = project(hp, pos_p, 1024)
    kv_p = matmul(ckvb_p, w_kv, tm=1024, tn=512, out_dtype=BF16, name="mla_kv_up")
    o_p = mla_prompt_attention(qn_p, qpe_p, kv_p, kpe_p, nbatch=nbatch, seq=seq)
    xp = matmul(o_p, w_o, xp, tm=1024, tn=512, name="mla_wo_p")
    m_s = hs.shape[0]
    qn_s, qpe_s, ckv_s, _, kpe_s = project(hs, pos_s, m_s)
    w_uk_t = w_uk.transpose(1, 2, 0)
    q_lat = head_matmul(qn_s, w_uk_t, out_dtype=BF16, name="mla_q_lat")
    r = t * h
    q_lat = q_lat.reshape(t, ns, h, MLA_KV_LORA).transpose(1, 0, 2, 3).reshape(ns, r, MLA_KV_LORA)
    q_pe = qpe_s.reshape(t, ns, h, MLA_ROPE).transpose(1, 0, 2, 3).reshape(ns, r, MLA_ROPE)
    tpos_rows = jnp.asarray((past_len + np.arange(r) // h).astype(np.float32).reshape(-1, 1))
    o_lat = mla_sample_attention(page_table, cache_ckv, cache_kpe, q_lat, q_pe,
                                 _pad_rows8(ckv_s, ns), _pad_rows8(kpe_s, ns), tpos_rows, past_len=past_len)
    o_lat = o_lat.reshape(ns, t, h, MLA_KV_LORA).transpose(1, 0, 2, 3).reshape(t * ns, h * MLA_KV_LORA)
    o_s = head_matmul(o_lat, w_uv.transpose(1, 0, 2), out_dtype=BF16, name="mla_o_up")
    xs = matmul(o_s, w_o, xs, tm=m_s, tn=512, name="mla_wo_s")
    return xp, xs, ckv_p, kpe_p, ckv_s, kpe_s


def _conv_ffn(xp, xs, g_norm, state_s, w_gate, w_up, conv_w, conv_b, w_down, *, layer, nbatch, seq, ns):
    hp = rmsnorm_rows(xp, g_norm)
    hs = rmsnorm_rows(xs, g_norm)
    w_down_b = w_down.astype(BF16)
    zero_state = jnp.zeros((nbatch, CONV_W - 1, w_gate.shape[2]), F32)
    act_p, st_p = ffn_up_prompt(hp, w_gate, w_up, conv_w, conv_b, zero_state, layer=layer, seq=seq)
    xp = matmul(act_p, w_down_b, xp, tm=512, tn=512, name="ffn_down_p")
    act_s, st_s = ffn_up_sample(hs, w_gate, w_up, conv_w, conv_b, state_s.transpose(1, 0, 2), layer=layer, nseq=ns)
    xs = matmul(act_s, w_down_b, xs, tm=xs.shape[0], tn=512, name="ffn_down_s")
    return xp, xs, st_p, st_s.transpose(1, 0, 2)


def kernel(x_prompt, x_sample, cache_moba_k, cache_moba_v, cache_mla_ckv, cache_mla_kpe, state_ffn_conv, page_table, g_mix_norm, g_ffn_norm, moba_w_qkv, moba_g_q, moba_g_k, moba_w_o, mla_w_down, mla_g_cq, mla_w_uq, mla_g_q, mla_g_ckv, mla_g_kpe, mla_w_uk, mla_w_uv, mla_w_o, ffn_w_gate, ffn_w_up, ffn_conv_w, ffn_conv_b, ffn_w_down):
    nbatch, seq, d = x_prompt.shape
    ns, t, _ = x_sample.shape
    n_pool = cache_moba_k.shape[1]
    past_len = page_table.shape[1] * PAGE_SIZE
    depth = g_mix_norm.shape[0]
    dims = dict(nbatch=nbatch, seq=seq, ns=ns, t=t, past_len=past_len)
    xp = x_prompt.reshape(nbatch * seq, d)
    xs = x_sample.transpose(1, 0, 2).reshape(t * ns, d)
    outs = {k: [] for k in ("mk_p", "mv_p", "mk_s", "mv_s", "mc_p", "mr_p", "mc_s", "mr_s", "cv_p", "cv_s")}

    def seq_major(x):
        return x.reshape(t, ns, -1).transpose(1, 0, 2)

    for i in range(depth):
        hp = rmsnorm_rows(xp, g_mix_norm[i])
        hs = rmsnorm_rows(xs, g_mix_norm[i])
        j = i // 2
        if i % 2 == 0:
            ck = cache_moba_k[j].reshape(n_pool, PAGE_SIZE * MOBA_KV_HEADS, HEAD_DIM)
            cv = cache_moba_v[j].reshape(n_pool, PAGE_SIZE * MOBA_KV_HEADS, HEAD_DIM)
            xp, xs, kp, vp, ks, vs = _moba_layer(hp, hs, xp, xs, ck, cv, page_table, moba_w_qkv[j], moba_g_q[j],
                                                 moba_g_k[j], moba_w_o[j], **dims)
            outs["mk_p"].append(kp.reshape(nbatch, seq, MOBA_KV_HEADS, HEAD_DIM))
            outs["mv_p"].append(vp.reshape(nbatch, seq, MOBA_KV_HEADS, HEAD_DIM))
            outs["mk_s"].append(seq_major(ks).reshape(ns, t, MOBA_KV_HEADS, HEAD_DIM))
            outs["mv_s"].append(seq_major(vs).reshape(ns, t, MOBA_KV_HEADS, HEAD_DIM))
        else:
            xp, xs, cp, rp, cs, rs = _mla_layer(hp, hs, xp, xs, cache_mla_ckv[j],
                                                jnp.swapaxes(cache_mla_kpe[j], 1, 2), page_table,
                                                mla_w_down[j], mla_g_cq[j], mla_w_uq[j], mla_g_q[j], mla_g_ckv[j],
                                                mla_g_kpe[j], mla_w_uk[j], mla_w_uv[j], mla_w_o[j], **dims)
            outs["mc_p"].append(cp.reshape(nbatch, seq, -1))
            outs["mr_p"].append(rp.reshape(nbatch, seq, -1))
            outs["mc_s"].append(seq_major(cs))
            outs["mr_s"].append(seq_major(rs))
        xp, xs, st_p, st_s = _conv_ffn(xp, xs, g_ffn_norm[i], state_ffn_conv[i], ffn_w_gate, ffn_w_up,
                                       ffn_conv_w[i], ffn_conv_b[i], ffn_w_down[i], layer=i, nbatch=nbatch, seq=seq,
                                       ns=ns)
        outs["cv_p"].append(st_p)
        outs["cv_s"].append(st_s)
    y_p = xp.reshape(nbatch, seq, d)
    y_s = seq_major(xs)
    return (y_p, y_s, jnp.stack(outs["mk_p"]), jnp.stack(outs["mv_p"]), jnp.stack(outs["mk_s"]),
            jnp.stack(outs["mv_s"]), jnp.stack(outs["mc_p"]), jnp.stack(outs["mr_p"]), jnp.stack(outs["mc_s"]),
            jnp.stack(outs["mr_s"]), jnp.stack(outs["cv_p"]), jnp.stack(outs["cv_s"]))
```

```python
import functools

import numpy as np
import jax
import jax.numpy as jnp
from jax import lax
from jax.experimental import pallas as pl
from jax.experimental.pallas import tpu as pltpu

F32 = jnp.float32
BF16 = jnp.bfloat16

D_MODEL = 4096
HEAD_DIM = 128
MOBA_Q_HEADS = 32
MOBA_KV_HEADS = 2
MOBA_GROUP = 16
MOBA_BLOCK = 256
MOBA_TOPK = 3
MLA_HEADS = 32
MLA_Q_LORA = 1024
MLA_KV_LORA = 512
MLA_NOPE = 128
MLA_ROPE = 64
MLA_QK_DIM = MLA_NOPE + MLA_ROPE
MLA_V_DIM = 128
ROPE_THETA = 10000.0
CONV_W = 3
NORM_EPS = 1e-6
PAGE_SIZE = 128

NEG = -1e30
VMEM_LIMIT = 56 * 1024 * 1024
NT = (((1,), (1,)), ((), ()))


def _params(sem):
    return pltpu.CompilerParams(dimension_semantics=sem, vmem_limit_bytes=VMEM_LIMIT)


def _rmsnorm_kernel(x_ref, g_ref, o_ref):
    x = x_ref[...]
    y = x * lax.rsqrt(jnp.mean(x * x, axis=-1, keepdims=True) + NORM_EPS)
    o_ref[...] = (y * g_ref[...]).astype(o_ref.dtype)


def rmsnorm_rows(x, g, tr=256):
    m, d = x.shape
    return pl.pallas_call(
        _rmsnorm_kernel,
        grid=(m // tr,),
        in_specs=[pl.BlockSpec((tr, d), lambda i: (i, 0)),
                  pl.BlockSpec((1, d), lambda i: (0, 0))],
        out_specs=pl.BlockSpec((tr, d), lambda i: (i, 0)),
        out_shape=jax.ShapeDtypeStruct((m, d), BF16),
        compiler_params=_params(("parallel",)),
        name="rmsnorm",
    )(x, g.reshape(1, d))


def _mm_kernel(a_ref, w_ref, *rest, has_res):
    o_ref = rest[-1]
    acc = jnp.dot(a_ref[...].astype(BF16), w_ref[...].astype(BF16), preferred_element_type=F32)
    if has_res:
        acc = rest[0][...] + acc
    o_ref[...] = acc.astype(o_ref.dtype)


def matmul(a, w, res=None, *, tm, tn, out_dtype=F32, name="matmul"):
    m, k = a.shape
    n = w.shape[1]
    assert m % tm == 0 and n % tn == 0, (m, n, tm, tn)
    in_specs = [pl.BlockSpec((tm, k), lambda i, j: (i, 0)),
                pl.BlockSpec((k, tn), lambda i, j: (0, j))]
    args = [a, w]
    if res is not None:
        in_specs.append(pl.BlockSpec((tm, tn), lambda i, j: (i, j)))
        args.append(res)
    return pl.pallas_call(
        functools.partial(_mm_kernel, has_res=res is not None),
        grid=(m // tm, n // tn),
        in_specs=in_specs,
        out_specs=pl.BlockSpec((tm, tn), lambda i, j: (i, j)),
        out_shape=jax.ShapeDtypeStruct((m, n), out_dtype),
        compiler_params=_params(("parallel", "parallel")),
        name=name,
    )(*args)


def _hmm_kernel(a_ref, w_ref, o_ref):
    o_ref[...] = jnp.dot(a_ref[...].astype(BF16), w_ref[...].astype(BF16),
                         preferred_element_type=F32).astype(o_ref.dtype)


def head_matmul(a, w, *, out_dtype, name):
    m = a.shape[0]
    nh, ka, n = w.shape
    return pl.pallas_call(
        _hmm_kernel,
        grid=(nh,),
        in_specs=[pl.BlockSpec((m, ka), lambda h: (0, h)),
                  pl.BlockSpec((None, ka, n), lambda h: (h, 0, 0))],
        out_specs=pl.BlockSpec((m, n), lambda h: (0, h)),
        out_shape=jax.ShapeDtypeStruct((m, nh * n), out_dtype),
        compiler_params=_params(("parallel",)),
        name=name,
    )(a, w)


def _silu_mul(gc, u):
    return (gc * (1.0 / (1.0 + jnp.exp(-gc)))) * u


def _ffn_up_prompt_kernel(a_ref, wg_ref, wu_ref, cw_ref, cb_ref, st_ref, act_ref, ns_ref):
    a = a_ref[...]
    g = jnp.dot(a, wg_ref[...].astype(BF16), preferred_element_type=F32)
    u = jnp.dot(a, wu_ref[...].astype(BF16), preferred_element_type=F32)
    tm = g.shape[0]
    row = lax.broadcasted_iota(jnp.int32, g.shape, 0)
    st0 = st_ref[0, 0:1, :]
    st1 = st_ref[0, 1:2, :]
    g1 = jnp.where(row >= 1, pltpu.roll(g, 1, 0), st1)
    g2 = jnp.where(row >= 2, pltpu.roll(g, 2, 0), jnp.where(row == 1, st1, st0))
    cw = cw_ref[...]
    gc = cb_ref[...] + cw[0:1, :] * g2
    gc = gc + cw[1:2, :] * g1
    gc = gc + cw[2:3, :] * g
    act_ref[...] = _silu_mul(gc, u).astype(act_ref.dtype)
    ns_ref[0] = g[tm - 2:tm, :]


def ffn_up_prompt(h, wg, wu, cw, cb, state, *, layer, seq, tn=256):
    m, k = h.shape
    n = wg.shape[2]
    nb = m // seq
    return pl.pallas_call(
        _ffn_up_prompt_kernel,
        grid=(nb, n // tn),
        in_specs=[pl.BlockSpec((seq, k), lambda i, j: (i, 0), pipeline_mode=pl.Buffered(1)),
                  pl.BlockSpec((None, k, tn), lambda i, j: (layer, 0, j)),
                  pl.BlockSpec((None, k, tn), lambda i, j: (layer, 0, j)),
                  pl.BlockSpec((CONV_W, tn), lambda i, j: (0, j)),
                  pl.BlockSpec((1, tn), lambda i, j: (0, j)),
                  pl.BlockSpec((1, 2, tn), lambda i, j: (i, 0, j))],
        out_specs=[pl.BlockSpec((seq, tn), lambda i, j: (i, j)),
                   pl.BlockSpec((1, 2, tn), lambda i, j: (i, 0, j))],
        out_shape=[jax.ShapeDtypeStruct((m, n), BF16),
                   jax.ShapeDtypeStruct((nb, 2, n), F32)],
        compiler_params=_params(("parallel", "parallel")),
        name="ffn_up_prompt",
    )(h, wg, wu, cw, cb.reshape(1, n), state)


def _ffn_up_sample_kernel(a_ref, wg_ref, wu_ref, cw_ref, cb_ref, st_ref, act_ref, ns_ref, *, nseq):
    a = a_ref[...]
    g = jnp.dot(a, wg_ref[...].astype(BF16), preferred_element_type=F32)
    u = jnp.dot(a, wu_ref[...].astype(BF16), preferred_element_type=F32)
    tm = g.shape[0]
    st0 = st_ref[0]
    st1 = st_ref[1]
    g1 = jnp.concatenate([st1, g[:tm - nseq]], axis=0)
    g2 = jnp.concatenate([st0, st1, g[:tm - 2 * nseq]], axis=0)
    cw = cw_ref[...]
    gc = cb_ref[...] + cw[0:1, :] * g2
    gc = gc + cw[1:2, :] * g1
    gc = gc + cw[2:3, :] * g
    act_ref[...] = _silu_mul(gc, u).astype(act_ref.dtype)
    ns_ref[0] = g[tm - 2 * nseq:tm - nseq]
    ns_ref[1] = g[tm - nseq:]


def ffn_up_sample(h, wg, wu, cw, cb, state_t, *, layer, nseq, tn=256):
    m, k = h.shape
    n = wg.shape[2]
    return pl.pallas_call(
        functools.partial(_ffn_up_sample_kernel, nseq=nseq),
        grid=(n // tn,),
        in_specs=[pl.BlockSpec((m, k), lambda j: (0, 0)),
                  pl.BlockSpec((None, k, tn), lambda j: (layer, 0, j)),
                  pl.BlockSpec((None, k, tn), lambda j: (layer, 0, j)),
                  pl.BlockSpec((CONV_W, tn), lambda j: (0, j)),
                  pl.BlockSpec((1, tn), lambda j: (0, j)),
                  pl.BlockSpec((2, nseq, tn), lambda j: (0, 0, j))],
        out_specs=[pl.BlockSpec((m, tn), lambda j: (0, j)),
                   pl.BlockSpec((2, nseq, tn), lambda j: (0, 0, j))],
        out_shape=[jax.ShapeDtypeStruct((m, n), BF16),
                   jax.ShapeDtypeStruct((2, nseq, n), F32)],
        compiler_params=_params(("parallel",)),
        name="ffn_up_sample",
    )(h, wg, wu, cw, cb.reshape(1, n), state_t)


def _moba_qknorm_kernel(x_ref, gq_ref, gk_ref, q_ref, k_ref, ka_ref, km_ref, *, nblk):
    gq = gq_ref[...]
    gk = gk_ref[...]
    for h in range(MOBA_Q_HEADS):
        x = x_ref[:, h * HEAD_DIM:(h + 1) * HEAD_DIM]
        y = x * lax.rsqrt(jnp.mean(x * x, axis=-1, keepdims=True) + NORM_EPS)
        q_ref[:, h * HEAD_DIM:(h + 1) * HEAD_DIM] = y * gq
    lane = lax.broadcasted_iota(jnp.int32, (x_ref.shape[0], HEAD_DIM), 1)
    onehot = jnp.where(lane == pl.program_id(0) % nblk, 1.0, 0.0).astype(ka_ref.dtype)
    for j in range(MOBA_KV_HEADS):
        c0 = (MOBA_Q_HEADS + j) * HEAD_DIM
        x = x_ref[:, c0:c0 + HEAD_DIM]
        y = (x * lax.rsqrt(jnp.mean(x * x, axis=-1, keepdims=True) + NORM_EPS)) * gk
        k_ref[:, j * HEAD_DIM:(j + 1) * HEAD_DIM] = y
        ka_ref[:, 2 * j * HEAD_DIM:(2 * j + 1) * HEAD_DIM] = y.astype(ka_ref.dtype)
        ka_ref[:, (2 * j + 1) * HEAD_DIM:(2 * j + 2) * HEAD_DIM] = onehot
        km_ref[0, :, j * HEAD_DIM:(j + 1) * HEAD_DIM] = jnp.mean(y, axis=0, keepdims=True)


def moba_qknorm(qkv, gq, gk, *, nblk):
    m, n = qkv.shape
    tr = MOBA_BLOCK
    nq = MOBA_Q_HEADS * HEAD_DIM
    nk = MOBA_KV_HEADS * HEAD_DIM
    return pl.pallas_call(
        functools.partial(_moba_qknorm_kernel, nblk=nblk),
        grid=(m // tr,),
        in_specs=[pl.BlockSpec((tr, n), lambda i: (i, 0)),
                  pl.BlockSpec((1, HEAD_DIM), lambda i: (0, 0)),
                  pl.BlockSpec((1, HEAD_DIM), lambda i: (0, 0))],
        out_specs=[pl.BlockSpec((tr, nq), lambda i: (i, 0)),
                   pl.BlockSpec((tr, nk), lambda i: (i, 0)),
                   pl.BlockSpec((tr, 2 * nk), lambda i: (i, 0)),
                   pl.BlockSpec((1, 1, nk), lambda i: (i, 0, 0))],
        out_shape=[jax.ShapeDtypeStruct((m, nq), F32),
                   jax.ShapeDtypeStruct((m, nk), F32),
                   jax.ShapeDtypeStruct((m, 2 * nk), BF16),
                   jax.ShapeDtypeStruct((m // tr, 1, nk), F32)],
        compiler_params=_params(("parallel",)),
        name="moba_qknorm",
    )(qkv, gq.reshape(1, HEAD_DIM), gk.reshape(1, HEAD_DIM))


def _top3(gate, allowed, axis=1):
    nb = gate.shape[axis]
    pos = lax.broadcasted_iota(jnp.int32, gate.shape, axis)
    gm = jnp.where(allowed, gate, -jnp.inf)
    sel = jnp.zeros(gate.shape, F32)
    for _ in range(MOBA_TOPK):
        mx = jnp.max(gm, axis=axis, keepdims=True)
        idx = jnp.min(jnp.where(gm == mx, pos, nb), axis=axis, keepdims=True)
        pick = pos == idx
        sel = jnp.where(pick, 1.0, sel)
        gm = jnp.where(pick, -jnp.inf, gm)
    return jnp.where(allowed, sel, 0.0)


def _online_update(carry, s, ok, v):
    m, l, acc = carry
    s = jnp.where(ok, s, NEG)
    m_new = jnp.maximum(m, jnp.max(s, axis=1, keepdims=True))
    p = jnp.where(ok, jnp.exp(s - m_new), 0.0)
    alpha = jnp.exp(m - m_new)
    l = alpha * l + jnp.sum(p, axis=1, keepdims=True)
    acc = alpha * acc + jnp.dot(p.astype(BF16), v, preferred_element_type=F32)
    return m_new, l, acc


def _moba_prompt_kernel(slopes_ref, q_ref, ka_ref, vt_ref, km_ref, o_ref, *, hstack, nbp):
    kvh = pl.program_id(1)
    qt = pl.program_id(2)
    blk = MOBA_BLOCK
    cols = hstack * blk
    scale = HEAD_DIM ** -0.5
    km = km_ref[0, 0]
    bidx = lax.broadcasted_iota(jnp.int32, (nbp, cols), 0)
    cr = (lax.broadcasted_iota(jnp.int32, (blk, cols), 0)
          - lax.broadcasted_iota(jnp.int32, (blk, cols), 1) % blk)
    crf = cr.astype(F32)
    off_own = pl.multiple_of(qt * blk, blk)
    ka_own = ka_ref[pl.ds(off_own, blk), :]
    vt_own = vt_ref[0, 0, qt]
    pad = jnp.zeros((HEAD_DIM - nbp, cols), F32)
    for hg in range(MOBA_GROUP // hstack):
        heads = [hg * hstack + i for i in range(hstack)]
        qs = jnp.concatenate([q_ref[:, h * HEAD_DIM:(h + 1) * HEAD_DIM] for h in heads], axis=0)
        slope = jnp.concatenate([jnp.full((1, blk), slopes_ref[kvh * MOBA_GROUP + h], F32) for h in heads], axis=1)
        gate = lax.dot_general(km, qs, NT, precision=lax.Precision.HIGHEST, preferred_element_type=F32)
        sel = _top3(gate, bidx < qt, axis=0)
        blockbias = jnp.where((sel > 0.5) | (bidx == qt), 0.0, NEG)
        qa = jnp.concatenate([(qs * scale).T, blockbias, pad], axis=0).astype(BF16)
        base = slope * crf
        s = jnp.dot(ka_own, qa, preferred_element_type=F32) + base
        s = jnp.where(cr <= 0, s, NEG)
        m = jnp.max(s, axis=0, keepdims=True)
        p = jnp.exp(s - m)
        l = jnp.sum(p, axis=0, keepdims=True)
        acc = jnp.dot(vt_own, p.astype(BF16), preferred_element_type=F32)

        def body(n, carry, qa=qa, base=base, slope=slope):
            m, l, acc = carry
            off = pl.multiple_of(n * blk, blk)
            s = jnp.dot(ka_ref[pl.ds(off, blk), :], qa, preferred_element_type=F32) + base
            qterm = slope * (jnp.zeros((1, cols), jnp.int32) + (n - qt) * blk).astype(F32)
            m_new = jnp.maximum(m, jnp.max(s, axis=0, keepdims=True) + qterm)
            p = jnp.exp(s - (m_new - qterm))
            alpha = jnp.exp(m - m_new)
            l = alpha * l + jnp.sum(p, axis=0, keepdims=True)
            acc = alpha * acc + jnp.dot(vt_ref[0, 0, n], p.astype(BF16), preferred_element_type=F32)
            return m_new, l, acc

        m, l, acc = lax.fori_loop(0, qt, body, (m, l, acc))
        o = (acc / l).T
        for i, h in enumerate(heads):
            o_ref[:, h * HEAD_DIM:(h + 1) * HEAD_DIM] = o[i * blk:(i + 1) * blk].astype(o_ref.dtype)


def moba_prompt_attention(qn, ka, vt, kmean, slopes, *, nbatch, seq, hstack=4):
    m = qn.shape[0]
    nqt = seq // MOBA_BLOCK
    nbp = kmean.shape[2]
    assert nqt <= nbp <= HEAD_DIM and nbp % 8 == 0
    gw = MOBA_GROUP * HEAD_DIM
    return pl.pallas_call(
        functools.partial(_moba_prompt_kernel, hstack=hstack, nbp=nbp),
        grid=(nbatch, MOBA_KV_HEADS, nqt),
        in_specs=[pl.BlockSpec(memory_space=pltpu.SMEM),
                  pl.BlockSpec((MOBA_BLOCK, gw), lambda b, k, t: (b * nqt + t, k)),
                  pl.BlockSpec((seq, 2 * HEAD_DIM), lambda b, k, t: (b, k)),
                  pl.BlockSpec((1, 1, nqt, HEAD_DIM, MOBA_BLOCK), lambda b, k, t: (b, k, 0, 0, 0)),
                  pl.BlockSpec((1, 1, nbp, HEAD_DIM), lambda b, k, t: (b, k, 0, 0))],
        out_specs=pl.BlockSpec((MOBA_BLOCK, gw), lambda b, k, t: (b * nqt + t, k)),
        out_shape=jax.ShapeDtypeStruct((m, MOBA_Q_HEADS * HEAD_DIM), BF16),
        compiler_params=_params(("parallel", "parallel", "parallel")),
        name="moba_prompt_attn",
    )(slopes, qn, ka, vt, kmean)


def _kv_chunk(buf, slot, pps):
    pages = [jnp.concatenate([buf[slot, p, pl.ds(j, PAGE_SIZE, stride=MOBA_KV_HEADS), :]
                              for j in range(MOBA_KV_HEADS)], axis=1) for p in range(pps)]
    return jnp.concatenate(pages, axis=0)


def _moba_sample_kernel(pt_ref, ck_hbm, cv_hbm, q_ref, kn_ref, vn_ref, slope_ref, tpos_ref, o_ref,
                        kvbuf, ksem, vsem, s_ref, km_ref, sel_ref, m_ref, l_ref, acc_ref,
                        *, pps, nchunk, npages, past_len):
    c = pl.program_id(1)
    slot = _page_gather(pt_ref, [ck_hbm, cv_hbm], [kvbuf, kvbuf], [ksem, vsem], nsteps=2 * nchunk, npages=npages,
                        pps=pps, chunk_of=lambda st: [(0, st, st < nchunk), (1, st - nchunk, st >= nchunk)])
    scale = HEAD_DIM ** -0.5
    ck = pps * PAGE_SIZE
    bpc = ck // MOBA_BLOCK
    nblk = nchunk * bpc
    slope = slope_ref[...]
    tpos = tpos_ref[...]

    @pl.when(c < nchunk)
    def _scores():
        kc = _kv_chunk(kvbuf, slot, pps)
        s = lax.dot_general(q_ref[0].astype(BF16), kc.astype(BF16), NT, preferred_element_type=F32) * scale
        kpos = (c * ck + lax.broadcasted_iota(jnp.int32, s.shape, 1)).astype(F32)
        s_ref[c] = s - slope * (tpos - kpos)
        km_ref[c] = jnp.sum(kc.reshape(bpc, MOBA_BLOCK, kc.shape[1]), axis=1) * (1.0 / MOBA_BLOCK)

    @pl.when(c == nchunk)
    def _select():
        km = km_ref[...].reshape(nblk, km_ref.shape[2])
        gate = lax.dot_general(q_ref[0], km, NT, precision=lax.Precision.HIGHEST,
                               preferred_element_type=F32)
        sel_ref[...] = _top3(gate, jnp.full(gate.shape, True))
        m_ref[...] = jnp.full(m_ref.shape, NEG, F32)
        l_ref[...] = jnp.zeros(l_ref.shape, F32)
        acc_ref[...] = jnp.zeros(acc_ref.shape, F32)

    @pl.when(c >= nchunk)
    def _attend():
        cc = c - nchunk
        s = s_ref[cc]
        brow = lax.broadcasted_iota(jnp.int32, (nblk, ck), 0)
        bcol = cc * bpc + lax.broadcasted_iota(jnp.int32, (nblk, ck), 1) // MOBA_BLOCK
        expand = jnp.where(brow == bcol, 1.0, 0.0).astype(BF16)
        ok = jnp.dot(sel_ref[...].astype(BF16), expand, preferred_element_type=F32) > 0.5
        vc = _kv_chunk(kvbuf, slot, pps).astype(BF16)
        m, l, acc = _online_update((m_ref[...], l_ref[...], acc_ref[...]), s, ok, vc)
        m_ref[...] = m
        l_ref[...] = l
        acc_ref[...] = acc

    @pl.when(c == 2 * nchunk - 1)
    def _own_block():
        kn = kn_ref[0]
        s = lax.dot_general(q_ref[0].astype(BF16), kn.astype(BF16), NT, preferred_element_type=F32) * scale
        j = lax.broadcasted_iota(jnp.int32, s.shape, 1).astype(F32)
        dist = tpos - (past_len + j)
        s = s - slope * dist
        m, l, acc = _online_update((m_ref[...], l_ref[...], acc_ref[...]), s, dist >= 0,
                                   vn_ref[0].astype(BF16))
        o_ref[0] = acc / l


def moba_sample_attention(page_table, cache_k, cache_v, q_bd, k_new, v_new, slope_rows, tpos_rows,
                          *, past_len, pps=32):
    ns, r, w = q_bd.shape
    ck = pps * PAGE_SIZE
    assert past_len % MOBA_BLOCK == 0 and ck % MOBA_BLOCK == 0
    npages = past_len // PAGE_SIZE
    nchunk = npages // pps
    nblk = past_len // MOBA_BLOCK
    prows = PAGE_SIZE * MOBA_KV_HEADS
    per_seq = lambda s, c, pt: (s, 0, 0)
    const2 = lambda s, c, pt: (0, 0)
    grid_spec = pltpu.PrefetchScalarGridSpec(
        num_scalar_prefetch=1,
        grid=(ns, 2 * nchunk),
        in_specs=[
            pl.BlockSpec(memory_space=pl.ANY),
            pl.BlockSpec(memory_space=pl.ANY),
            pl.BlockSpec((1, r, w), per_seq),
            pl.BlockSpec((1, 8, w), per_seq),
            pl.BlockSpec((1, 8, w), per_seq),
            pl.BlockSpec((r, 1), const2),
            pl.BlockSpec((r, 1), const2)],
        out_specs=pl.BlockSpec((1, r, w), per_seq),
        scratch_shapes=[pltpu.VMEM((2, pps, prows, HEAD_DIM), F32),
                        pltpu.SemaphoreType.DMA((2,)),
                        pltpu.SemaphoreType.DMA((2,)),
                        pltpu.VMEM((nchunk, r, ck), F32),
                        pltpu.VMEM((nchunk, ck // MOBA_BLOCK, w), F32),
                        pltpu.VMEM((r, nblk), F32),
                        pltpu.VMEM((r, 1), F32),
                        pltpu.VMEM((r, 1), F32),
                        pltpu.VMEM((r, w), F32)])
    return pl.pallas_call(
        functools.partial(_moba_sample_kernel, pps=pps, nchunk=nchunk, npages=npages, past_len=past_len),
        grid_spec=grid_spec,
        out_shape=jax.ShapeDtypeStruct((ns, r, w), F32),
        compiler_params=_params(("arbitrary", "arbitrary")),
        name="moba_sample_attn",
    )(page_table.reshape(-1), cache_k, cache_v, q_bd, k_new, v_new, slope_rows, tpos_rows)


def _mla_kv_prep_kernel(d_ref, gcq_ref, gckv_ref, gk_ref, gks_ref, cos_ref, sin_ref,
                        cq_ref, ckv_ref, ckvb_ref, kpe_ref):
    c0 = MLA_Q_LORA
    c1 = c0 + MLA_KV_LORA
    x = d_ref[:, 0:c0]
    y = x * lax.rsqrt(jnp.mean(x * x, axis=-1, keepdims=True) + NORM_EPS)
    cq_ref[...] = (y * gcq_ref[...]).astype(cq_ref.dtype)
    x = d_ref[:, c0:c1]
    y = (x * lax.rsqrt(jnp.mean(x * x, axis=-1, keepdims=True) + NORM_EPS)) * gckv_ref[...]
    ckv_ref[...] = y
    ckvb_ref[...] = y.astype(ckvb_ref.dtype)
    x = d_ref[:, c1:c1 + 128][:, :MLA_ROPE]
    xs = d_ref[:, c1 + 128:c1 + 256][:, :MLA_ROPE]
    rstd = lax.rsqrt(jnp.mean(x * x, axis=-1, keepdims=True) + NORM_EPS)
    kpe_ref[...] = ((x * rstd) * gk_ref[...]) * cos_ref[...] + ((xs * rstd) * gks_ref[...]) * sin_ref[...]


def mla_kv_prep(d, g_cq, g_ckv, g_kpe, g_kpe_sw, cos2, sin2s, *, tr=256):
    m, n = d.shape
    tab_rows = cos2.shape[0]
    ntab = tab_rows // tr if tab_rows >= tr else 1
    row = lambda i: (i, 0)
    const = lambda i: (0, 0)
    tab = lambda i: (i % ntab, 0)
    return pl.pallas_call(
        _mla_kv_prep_kernel,
        grid=(m // tr,),
        in_specs=[pl.BlockSpec((tr, n), row),
                  pl.BlockSpec((1, MLA_Q_LORA), const),
                  pl.BlockSpec((1, MLA_KV_LORA), const),
                  pl.BlockSpec((1, MLA_ROPE), const),
                  pl.BlockSpec((1, MLA_ROPE), const),
                  pl.BlockSpec((tr, MLA_ROPE), tab),
                  pl.BlockSpec((tr, MLA_ROPE), tab)],
        out_specs=[pl.BlockSpec((tr, MLA_Q_LORA), row),
                   pl.BlockSpec((tr, MLA_KV_LORA), row),
                   pl.BlockSpec((tr, MLA_KV_LORA), row),
                   pl.BlockSpec((tr, MLA_ROPE), row)],
        out_shape=[jax.ShapeDtypeStruct((m, MLA_Q_LORA), BF16),
                   jax.ShapeDtypeStruct((m, MLA_KV_LORA), F32),
                   jax.ShapeDtypeStruct((m, MLA_KV_LORA), BF16),
                   jax.ShapeDtypeStruct((m, MLA_ROPE), F32)],
        compiler_params=_params(("parallel",)),
        name="mla_kv_prep",
    )(d, g_cq.reshape(1, -1), g_ckv.reshape(1, -1), g_kpe.reshape(1, -1), g_kpe_sw.reshape(1, -1), cos2, sin2s)


def _mla_q_prep_kernel(x_ref, gn_ref, gr_ref, grs_ref, cos_ref, sin_ref, qn_ref, qpe_ref):
    scale = MLA_QK_DIM ** -0.5
    nn = MLA_HEADS * MLA_NOPE
    nr = MLA_HEADS * MLA_ROPE
    gn = gn_ref[...]
    gr = gr_ref[...]
    grs = grs_ref[...]
    cos = cos_ref[...]
    sin = sin_ref[...]
    lane = lax.broadcasted_iota(jnp.int32, (x_ref.shape[0], 128), 1)
    first = lane < MLA_ROPE
    for hp in range(MLA_HEADS // 2):
        r = x_ref[:, nn + hp * 128:nn + (hp + 1) * 128]
        rs = x_ref[:, nn + nr + hp * 128:nn + nr + (hp + 1) * 128]
        r2 = r * r
        ss_a = jnp.sum(jnp.where(first, r2, 0.0), axis=-1, keepdims=True)
        ss_b = jnp.sum(jnp.where(first, 0.0, r2), axis=-1, keepdims=True)
        rstd_pair = []
        for sub, ss_r in ((0, ss_a), (1, ss_b)):
            h = 2 * hp + sub
            n = x_ref[:, h * MLA_NOPE:(h + 1) * MLA_NOPE]
            ss = jnp.sum(n * n, axis=-1, keepdims=True) + ss_r
            rstd = lax.rsqrt(ss * (1.0 / MLA_QK_DIM) + NORM_EPS)
            qn_ref[:, h * MLA_NOPE:(h + 1) * MLA_NOPE] = (((n * rstd) * gn) * scale).astype(qn_ref.dtype)
            rstd_pair.append(rstd)
        rstd = jnp.where(first, rstd_pair[0], rstd_pair[1])
        qpe = ((r * rstd) * gr) * cos + ((rs * rstd) * grs) * sin
        qpe_ref[:, hp * 128:(hp + 1) * 128] = (qpe * scale).astype(qpe_ref.dtype)


def mla_q_prep(x, g_n, g_r2, g_rs2, cos4, sin4s, *, tr=256):
    m, n = x.shape
    tab_rows = cos4.shape[0]
    ntab = tab_rows // tr if tab_rows >= tr else 1
    row = lambda i: (i, 0)
    const = lambda i: (0, 0)
    tab = lambda i: (i % ntab, 0)
    return pl.pallas_call(
        _mla_q_prep_kernel,
        grid=(m // tr,),
        in_specs=[pl.BlockSpec((tr, n), row),
                  pl.BlockSpec((1, MLA_NOPE), const),
                  pl.BlockSpec((1, 128), const),
                  pl.BlockSpec((1, 128), const),
                  pl.BlockSpec((tr, 128), tab),
                  pl.BlockSpec((tr, 128), tab)],
        out_specs=[pl.BlockSpec((tr, MLA_HEADS * MLA_NOPE), row),
                   pl.BlockSpec((tr, MLA_HEADS * MLA_ROPE), row)],
        out_shape=[jax.ShapeDtypeStruct((m, MLA_HEADS * MLA_NOPE), BF16),
                   jax.ShapeDtypeStruct((m, MLA_HEADS * MLA_ROPE), BF16)],
        compiler_params=_params(("parallel",)),
        name="mla_q_prep",
    )(x, g_n.reshape(1, -1), g_r2.reshape(1, -1), g_rs2.reshape(1, -1), cos4, sin4s)


def _mla_prompt_kernel(qn_ref, qpe_ref, kn_ref, kpe_ref, v_ref, o_ref, *, tq):
    qt = pl.program_id(2)
    lane = lax.broadcasted_iota(jnp.int32, (tq, 128), 1)
    qpe = qpe_ref[...].astype(F32)
    halves = (lane < MLA_ROPE, lane >= MLA_ROPE)
    qs = [jnp.concatenate([qn_ref[:, i * MLA_NOPE:(i + 1) * MLA_NOPE],
                           jnp.where(halves[i], qpe, 0.0).astype(BF16)], axis=1) for i in range(2)]
    rc = (lax.broadcasted_iota(jnp.int32, (tq, tq), 0) - lax.broadcasted_iota(jnp.int32, (tq, tq), 1))

    def kv_block(n):
        off = pl.multiple_of(n * tq, tq)
        kpe = kpe_ref[pl.ds(off, tq), :]
        ks = [jnp.concatenate([kn_ref[pl.ds(off, tq), i * MLA_NOPE:(i + 1) * MLA_NOPE], kpe], axis=1)
              for i in range(2)]
        vs = [v_ref[pl.ds(off, tq), i * MLA_V_DIM:(i + 1) * MLA_V_DIM] for i in range(2)]
        return ks, vs

    ks, vs = kv_block(qt)
    carry = []
    for i in range(2):
        s = lax.dot_general(qs[i], ks[i], NT, preferred_element_type=F32)
        s = jnp.where(rc >= 0, s, NEG)
        m = jnp.max(s, axis=1, keepdims=True)
        p = jnp.exp(s - m)
        carry += [m, jnp.sum(p, axis=1, keepdims=True), jnp.dot(p.astype(BF16), vs[i], preferred_element_type=F32)]

    def body(n, carry):
        ks, vs = kv_block(n)
        out = []
        for i in range(2):
            m, l, acc = carry[3 * i:3 * i + 3]
            s = lax.dot_general(qs[i], ks[i], NT, preferred_element_type=F32)
            m_new = jnp.maximum(m, jnp.max(s, axis=1, keepdims=True))
            p = jnp.exp(s - m_new)
            alpha = jnp.exp(m - m_new)
            out += [m_new, alpha * l + jnp.sum(p, axis=1, keepdims=True),
                    alpha * acc + jnp.dot(p.astype(BF16), vs[i], preferred_element_type=F32)]
        return tuple(out)

    carry = lax.fori_loop(0, qt, body, tuple(carry))
    for i in range(2):
        m, l, acc = carry[3 * i:3 * i + 3]
        o_ref[:, i * MLA_V_DIM:(i + 1) * MLA_V_DIM] = (acc / l).astype(o_ref.dtype)


def mla_prompt_attention(qn, qpe, kv, kpe, *, nbatch, seq, tq=512):
    m = qn.shape[0]
    nqt = seq // tq
    npair = MLA_HEADS // 2
    kpe = jnp.concatenate([kpe, kpe], axis=1).astype(BF16)
    return pl.pallas_call(
        functools.partial(_mla_prompt_kernel, tq=tq),
        grid=(nbatch, npair, nqt),
        in_specs=[pl.BlockSpec((tq, 2 * MLA_NOPE), lambda b, h, t: (b * nqt + t, h)),
                  pl.BlockSpec((tq, 2 * MLA_ROPE), lambda b, h, t: (b * nqt + t, h)),
                  pl.BlockSpec((seq, 2 * MLA_NOPE), lambda b, h, t: (b, h)),
                  pl.BlockSpec((seq, 2 * MLA_ROPE), lambda b, h, t: (b, 0)),
                  pl.BlockSpec((seq, 2 * MLA_V_DIM), lambda b, h, t: (b, npair + h))],
        out_specs=pl.BlockSpec((tq, 2 * MLA_V_DIM), lambda b, h, t: (b * nqt + t, h)),
        out_shape=jax.ShapeDtypeStruct((m, MLA_HEADS * MLA_V_DIM), BF16),
        compiler_params=_params(("parallel", "parallel", "parallel")),
        name="mla_prompt_attn",
    )(qn, qpe, kv, kpe, kv)


def _page_gather(pt_ref, srcs, bufs, sems, *, nsteps, npages, pps, chunk_of):
    seq, st = pl.program_id(0), pl.program_id(1)
    lin = seq * nsteps + st
    slot = jnp.bitwise_and(lin, 1)

    def copies(which, page, slot_, j):
        idx = list(range(len(srcs))) if which is None else [which]
        return [pltpu.make_async_copy(srcs[i].at[page], bufs[i].at[slot_, j], sems[i].at[slot_]) for i in idx]

    def issue(seq_, st_, slot_):
        for which, chunk, pred in chunk_of(st_):
            @pl.when(pred)
            def _(which=which, chunk=chunk):
                base = seq_ * npages + chunk * pps

                def body(j, carry):
                    for cp in copies(which, pt_ref[base + j], slot_, j):
                        cp.start()
                    return carry

                lax.fori_loop(0, pps, body, 0)

    @pl.when(lin == 0)
    def _first():
        issue(seq, st, slot)

    wrap = st + 1 == nsteps
    nseq, nst = jnp.where(wrap, seq + 1, seq), jnp.where(wrap, 0, st + 1)

    @pl.when(nseq < pl.num_programs(0))
    def _next():
        issue(nseq, nst, 1 - slot)

    for which, _, pred in chunk_of(st):
        @pl.when(pred)
        def _(which=which):
            def body(j, carry):
                for cp in copies(which, 0, slot, j):
                    cp.wait()
                return carry

            lax.fori_loop(0, pps, body, 0)

    return slot


def _mla_sample_kernel(pt_ref, ckv_hbm, kpe_hbm, ql_ref, qp_ref, cn_ref, rn_ref, tpos_ref, o_ref,
                       cbuf, rbuf, csem, rsem, m_ref, l_ref, acc_ref, *, pps, nchunk, npages, past_len, nsub):
    c = pl.program_id(1)
    slot = _page_gather(pt_ref, [ckv_hbm, kpe_hbm], [cbuf, rbuf], [csem, rsem], nsteps=nchunk, npages=npages,
                        pps=pps, chunk_of=lambda st: [(None, st, st >= 0)])
    ql = ql_ref[0]
    qp = qp_ref[0]

    @pl.when(c == 0)
    def _init():
        m_ref[...] = jnp.full(m_ref.shape, NEG, F32)
        l_ref[...] = jnp.zeros(l_ref.shape, F32)
        acc_ref[...] = jnp.zeros(acc_ref.shape, F32)

    m, l, acc = m_ref[...], l_ref[...], acc_ref[...]
    per = pps // nsub
    ccs, ss = [], []
    for sub in range(nsub):
        cc = cbuf[slot, sub * per:(sub + 1) * per].reshape(per * PAGE_SIZE, MLA_KV_LORA).astype(BF16)
        rr = jnp.concatenate([rbuf[slot, j] for j in range(sub * per, (sub + 1) * per)], axis=1).astype(BF16)
        ccs.append(cc)
        ss.append(lax.dot_general(ql, cc, NT, preferred_element_type=F32)
                  + jnp.dot(qp, rr, preferred_element_type=F32))
    for s, cc in zip(ss, ccs):
        m_new = jnp.maximum(m, jnp.max(s, axis=1, keepdims=True))
        p = jnp.exp(s - m_new)
        alpha = jnp.exp(m - m_new)
        l = alpha * l + jnp.sum(p, axis=1, keepdims=True)
        acc = alpha * acc + jnp.dot(p.astype(BF16), cc, preferred_element_type=F32)
        m = m_new
    m_ref[...] = m
    l_ref[...] = l
    acc_ref[...] = acc

    @pl.when(c == nchunk - 1)
    def _new_rows():
        cn = cn_ref[0].astype(BF16)
        rn = rn_ref[0].astype(BF16)
        s2 = (lax.dot_general(ql, cn, NT, preferred_element_type=F32)
              + lax.dot_general(qp, rn, NT, preferred_element_type=F32))
        j = lax.broadcasted_iota(jnp.int32, s2.shape, 1).astype(F32)
        ok = (past_len + j) <= tpos_ref[...]
        m2, l2, acc2 = _online_update((m_ref[...], l_ref[...], acc_ref[...]), s2, ok, cn)
        o_ref[0] = (acc2 / l2).astype(o_ref.dtype)


def mla_sample_attention(page_table, cache_ckv, cache_kpe, q_lat, q_pe, c_new, r_new, tpos_rows,
                         *, past_len, pps=32, nsub=4):
    ns, r, _ = q_lat.shape
    npages = past_len // PAGE_SIZE
    nchunk = npages // pps
    per_seq = lambda s, c, pt: (s, 0, 0)
    grid_spec = pltpu.PrefetchScalarGridSpec(
        num_scalar_prefetch=1,
        grid=(ns, nchunk),
        in_specs=[
            pl.BlockSpec(memory_space=pl.ANY),
            pl.BlockSpec(memory_space=pl.ANY),
            pl.BlockSpec((1, r, MLA_KV_LORA), per_seq),
            pl.BlockSpec((1, r, MLA_ROPE), per_seq),
            pl.BlockSpec((1, 8, MLA_KV_LORA), per_seq),
            pl.BlockSpec((1, 8, MLA_ROPE), per_seq),
            pl.BlockSpec((r, 1), lambda s, c, pt: (0, 0))],
        out_specs=pl.BlockSpec((1, r, MLA_KV_LORA), per_seq),
        scratch_shapes=[pltpu.VMEM((2, pps, PAGE_SIZE, MLA_KV_LORA), F32),
                        pltpu.VMEM((2, pps, MLA_ROPE, PAGE_SIZE), F32),
                        pltpu.SemaphoreType.DMA((2,)),
                        pltpu.SemaphoreType.DMA((2,)),
                        pltpu.VMEM((r, 1), F32), pltpu.VMEM((r, 1), F32), pltpu.VMEM((r, MLA_KV_LORA), F32)])
    return pl.pallas_call(
        functools.partial(_mla_sample_kernel, pps=pps, nchunk=nchunk, npages=npages, past_len=past_len, nsub=nsub),
        grid_spec=grid_spec,
        out_shape=jax.ShapeDtypeStruct((ns, r, MLA_KV_LORA), BF16),
        compiler_params=_params(("arbitrary", "arbitrary")),
        name="mla_sample_attn",
    )(page_table.reshape(-1), cache_ckv, cache_kpe, q_lat, q_pe, c_new, r_new, tpos_rows)


def _rope_tables(pos, reps):
    half = MLA_ROPE // 2
    inv = ROPE_THETA ** (-jnp.arange(half, dtype=F32) / half)
    ang = pos.astype(F32)[:, None] * inv[None, :]
    cos, sin = jnp.cos(ang), jnp.sin(ang)
    cos2 = jnp.concatenate([cos, cos], axis=1)
    sin2 = jnp.concatenate([-sin, sin], axis=1)
    return jnp.tile(cos2, (1, reps)), jnp.tile(sin2, (1, reps))


def _swap_halves(x):
    half = x.shape[-1] // 2
    return jnp.concatenate([x[..., half:], x[..., :half]], axis=-1)


def _pad_rows8(x, ns):
    t = x.shape[0] // ns
    y = x.reshape(t, ns, x.shape[1]).transpose(1, 0, 2)
    return jnp.pad(y, ((0, 0), (0, 8 - t), (0, 0)))


def _moba_layer(hp, hs, xp, xs, cache_k, cache_v, page_table, w_qkv, g_q, g_k, w_o, *, nbatch, seq, ns, t, past_len):
    nq = MOBA_Q_HEADS * HEAD_DIM
    nk = MOBA_KV_HEADS * HEAD_DIM
    slopes = jnp.exp2(-8.0 * jnp.arange(1, MOBA_Q_HEADS + 1, dtype=F32) / MOBA_Q_HEADS)
    qkv_p = matmul(hp, w_qkv, tm=1024, tn=512, name="moba_qkv_p")
    nqt = seq // MOBA_BLOCK
    qn_p, kn_p, ka_p, km = moba_qknorm(qkv_p, g_q, g_k, nblk=nqt)
    kmean = km.reshape(nbatch, nqt, MOBA_KV_HEADS, HEAD_DIM).transpose(0, 2, 1, 3)
    kmean = jnp.pad(kmean, ((0, 0), (0, 0), (0, -nqt % 8), (0, 0)))
    vt_p = qkv_p[:, nq + nk:].astype(BF16).reshape(nbatch, nqt, MOBA_BLOCK, MOBA_KV_HEADS, HEAD_DIM)
    vt_p = vt_p.transpose(0, 3, 1, 4, 2)
    o_p = moba_prompt_attention(qn_p, ka_p, vt_p, kmean, slopes, nbatch=nbatch, seq=seq)
    xp = matmul(o_p, w_o, xp, tm=1024, tn=512, name="moba_wo_p")
    qkv_s = matmul(hs, w_qkv, tm=hs.shape[0], tn=512, name="moba_qkv_s")
    qn_s, kn_s, _, _ = moba_qknorm(qkv_s, g_q, g_k, nblk=1)
    vn_s = qkv_s[:, nq + nk:]
    r = t * MOBA_GROUP
    q5 = qn_s.reshape(t, ns, MOBA_KV_HEADS, MOBA_GROUP, HEAD_DIM).transpose(1, 2, 0, 3, 4)
    q5 = q5.reshape(ns, MOBA_KV_HEADS, r, HEAD_DIM)
    zeros = jnp.zeros_like(q5[:, 0])
    q_bd = jnp.concatenate([jnp.concatenate([q5[:, 0], zeros], axis=-1),
                            jnp.concatenate([zeros, q5[:, 1]], axis=-1)], axis=1)
    rows = np.arange(MOBA_KV_HEADS * r)
    row_kvh, row_t, row_g = rows // r, (rows % r) // MOBA_GROUP, rows % MOBA_GROUP
    slope_rows = slopes[row_kvh * MOBA_GROUP + row_g].reshape(-1, 1)
    tpos_rows = jnp.asarray((past_len + row_t).astype(np.float32).reshape(-1, 1))
    o_bd = moba_sample_attention(page_table, cache_k, cache_v, q_bd, _pad_rows8(kn_s, ns), _pad_rows8(vn_s, ns),
                                 slope_rows, tpos_rows, past_len=past_len)
    o5 = jnp.stack([o_bd[:, :r, :HEAD_DIM], o_bd[:, r:, HEAD_DIM:]], axis=1)
    o_s = o5.reshape(ns, MOBA_KV_HEADS, t, MOBA_GROUP, HEAD_DIM).transpose(2, 0, 1, 3, 4)
    o_s = o_s.reshape(t * ns, nq).astype(BF16)
    xs = matmul(o_s, w_o, xs, tm=o_s.shape[0], tn=512, name="moba_wo_s")
    return xp, xs, kn_p, qkv_p[:, nq + nk:], kn_s, vn_s


def _mla_layer(hp, hs, xp, xs, cache_ckv, cache_kpe, page_table, w_down, g_cq, w_uq, g_q, g_ckv, g_kpe,
               w_uk, w_uv, w_o, *, nbatch, seq, ns, t, past_len):
    h = MLA_HEADS
    c1 = MLA_Q_LORA + MLA_KV_LORA
    w_kpe = w_down[:, c1:]
    zpad = jnp.zeros((w_down.shape[0], 128 - MLA_ROPE), w_down.dtype)
    w_down_x = jnp.concatenate([w_down[:, :c1], w_kpe, zpad, _swap_halves(w_kpe), zpad], axis=1)
    w_uq3 = w_uq.reshape(MLA_Q_LORA, h, MLA_QK_DIM)
    w_rope = w_uq3[:, :, MLA_NOPE:]
    w_uq_x = jnp.concatenate([w_uq3[:, :, :MLA_NOPE].reshape(MLA_Q_LORA, -1),
                              w_rope.reshape(MLA_Q_LORA, -1),
                              _swap_halves(w_rope).reshape(MLA_Q_LORA, -1)], axis=1)
    w_kv = jnp.concatenate([w_uk.reshape(MLA_KV_LORA, -1), w_uv.reshape(MLA_KV_LORA, -1)], axis=1)
    g_n, g_r = g_q[:MLA_NOPE], g_q[MLA_NOPE:]
    g_r2, g_rs2 = jnp.tile(g_r, 2), jnp.tile(_swap_halves(g_r), 2)
    pos_p = jnp.arange(seq, dtype=jnp.int32)
    pos_s = jnp.repeat(past_len + jnp.arange(t, dtype=jnp.int32), ns)

    def project(hx, pos, tm):
        cos2, sin2 = _rope_tables(pos, 1)
        cos4, sin4 = _rope_tables(pos, 2)
        d = matmul(hx, w_down_x, tm=tm, tn=256, name="mla_down")
        cq, ckv, ckv_b, kpe = mla_kv_prep(d, g_cq, g_ckv, g_kpe, _swap_halves(g_kpe), cos2, sin2)
        qx = matmul(cq, w_uq_x, tm=tm, tn=512, name="mla_uq")
        qn, qpe = mla_q_prep(qx, g_n, g_r2, g_rs2, cos4, sin4)
        return qn, qpe, ckv, ckv_b, kpe

    qn_p, qpe_p, ckv_p, ckvb_p, kpe_p = project(hp, pos_p, 1024)
    kv_p = matmul(ckvb_p, w_kv, tm=1024, tn=512, out_dtype=BF16, name="mla_kv_up")
    o_p = mla_prompt_attention(qn_p, qpe_p, kv_p, kpe_p, nbatch=nbatch, seq=seq)
    xp = matmul(o_p, w_o, xp, tm=1024, tn=512, name="mla_wo_p")
    m_s = hs.shape[0]
    qn_s, qpe_s, ckv_s, _, kpe_s = project(hs, pos_s, m_s)
    w_uk_t = w_uk.transpose(1, 2, 0)
    q_lat = head_matmul(qn_s, w_uk_t, out_dtype=BF16, name="mla_q_lat")
    r = t * h
    q_lat = q_lat.reshape(t, ns, h, MLA_KV_LORA).transpose(1, 0, 2, 3).reshape(ns, r, MLA_KV_LORA)
    q_pe = qpe_s.reshape(t, ns, h, MLA_ROPE).transpose(1, 0, 2, 3).reshape(ns, r, MLA_ROPE)
    tpos_rows = jnp.asarray((past_len + np.arange(r) // h).astype(np.float32).reshape(-1, 1))
    o_lat = mla_sample_attention(page_table, cache_ckv, cache_kpe, q_lat, q_pe,
                                 _pad_rows8(ckv_s, ns), _pad_rows8(kpe_s, ns), tpos_rows, past_len=past_len)
    o_lat = o_lat.reshape(ns, t, h, MLA_KV_LORA).transpose(1, 0, 2, 3).reshape(t * ns, h * MLA_KV_LORA)
    o_s = head_matmul(o_lat, w_uv.transpose(1, 0, 2), out_dtype=BF16, name="mla_o_up")
    xs = matmul(o_s, w_o, xs, tm=m_s, tn=512, name="mla_wo_s")
    return xp, xs, ckv_p, kpe_p, ckv_s, kpe_s


def _conv_ffn(xp, xs, g_norm, state_s, w_gate, w_up, conv_w, conv_b, w_down, *, layer, nbatch, seq, ns):
    hp = rmsnorm_rows(xp, g_norm)
    hs = rmsnorm_rows(xs, g_norm)
    w_down_b = w_down.astype(BF16)
    zero_state = jnp.zeros((nbatch, CONV_W - 1, w_gate.shape[2]), F32)
    act_p, st_p = ffn_up_prompt(hp, w_gate, w_up, conv_w, conv_b, zero_state, layer=layer, seq=seq)
    xp = matmul(act_p, w_down_b, xp, tm=512, tn=512, name="ffn_down_p")
    act_s, st_s = ffn_up_sample(hs, w_gate, w_up, conv_w, conv_b, state_s.transpose(1, 0, 2), layer=layer, nseq=ns)
    xs = matmul(act_s, w_down_b, xs, tm=xs.shape[0], tn=512, name="ffn_down_s")
    return xp, xs, st_p, st_s.transpose(1, 0, 2)


def kernel(x_prompt, x_sample, cache_moba_k, cache_moba_v, cache_mla_ckv, cache_mla_kpe, state_ffn_conv, page_table, g_mix_norm, g_ffn_norm, moba_w_qkv, moba_g_q, moba_g_k, moba_w_o, mla_w_down, mla_g_cq, mla_w_uq, mla_g_q, mla_g_ckv, mla_g_kpe, mla_w_uk, mla_w_uv, mla_w_o, ffn_w_gate, ffn_w_up, ffn_conv_w, ffn_conv_b, ffn_w_down):
    nbatch, seq, d = x_prompt.shape
    ns, t, _ = x_sample.shape
    n_pool = cache_moba_k.shape[1]
    past_len = page_table.shape[1] * PAGE_SIZE
    depth = g_mix_norm.shape[0]
    dims = dict(nbatch=nbatch, seq=seq, ns=ns, t=t, past_len=past_len)
    xp = x_prompt.reshape(nbatch * seq, d)
    xs = x_sample.transpose(1, 0, 2).reshape(t * ns, d)
    outs = {k: [] for k in ("mk_p", "mv_p", "mk_s", "mv_s", "mc_p", "mr_p", "mc_s", "mr_s", "cv_p", "cv_s")}

    def seq_major(x):
        return x.reshape(t, ns, -1).transpose(1, 0, 2)

    for i in range(depth):
        hp = rmsnorm_rows(xp, g_mix_norm[i])
        hs = rmsnorm_rows(xs, g_mix_norm[i])
        j = i // 2
        if i % 2 == 0:
            ck = cache_moba_k[j].reshape(n_pool, PAGE_SIZE * MOBA_KV_HEADS, HEAD_DIM)
            cv = cache_moba_v[j].reshape(n_pool, PAGE_SIZE * MOBA_KV_HEADS, HEAD_DIM)
            xp, xs, kp, vp, ks, vs = _moba_layer(hp, hs, xp, xs, ck, cv, page_table, moba_w_qkv[j], moba_g_q[j],
                                                 moba_g_k[j], moba_w_o[j], **dims)
            outs["mk_p"].append(kp.reshape(nbatch, seq, MOBA_KV_HEADS, HEAD_DIM))
            outs["mv_p"].append(vp.reshape(nbatch, seq, MOBA_KV_HEADS, HEAD_DIM))
            outs["mk_s"].append(seq_major(ks).reshape(ns, t, MOBA_KV_HEADS, HEAD_DIM))
            outs["mv_s"].append(seq_major(vs).reshape(ns, t, MOBA_KV_HEADS, HEAD_DIM))
        else:
            xp, xs, cp, rp, cs, rs = _mla_layer(hp, hs, xp, xs, cache_mla_ckv[j],
                                                jnp.swapaxes(cache_mla_kpe[j], 1, 2), page_table,
                                                mla_w_down[j], mla_g_cq[j], mla_w_uq[j], mla_g_q[j], mla_g_ckv[j],
                                                mla_g_kpe[j], mla_w_uk[j], mla_w_uv[j], mla_w_o[j], **dims)
            outs["mc_p"].append(cp.reshape(nbatch, seq, -1))
            outs["mr_p"].append(rp.reshape(nbatch, seq, -1))
            outs["mc_s"].append(seq_major(cs))
            outs["mr_s"].append(seq_major(rs))
        xp, xs, st_p, st_s = _conv_ffn(xp, xs, g_ffn_norm[i], state_ffn_conv[i], ffn_w_gate, ffn_w_up,
                                       ffn_conv_w[i], ffn_conv_b[i], ffn_w_down[i], layer=i, nbatch=nbatch, seq=seq,
                                       ns=ns)
        outs["cv_p"].append(st_p)
        outs["cv_s"].append(st_s)
    y_p = xp.reshape(nbatch, seq, d)
    y_s = seq_major(xs)
    return (y_p, y_s, jnp.stack(outs["mk_p"]), jnp.stack(outs["mv_p"]), jnp.stack(outs["mk_s"]),
            jnp.stack(outs["mv_s"]), jnp.stack(outs["mc_p"]), jnp.stack(outs["mr_p"]), jnp.stack(outs["mc_s"]),
            jnp.stack(outs["mr_s"]), jnp.stack(outs["cv_p"]), jnp.stack(outs["cv_s"]))
```

```python
import functools

import numpy as np
import jax
import jax.numpy as jnp
from jax import lax
from jax.experimental import pallas as pl
from jax.experimental.pallas import tpu as pltpu

F32 = jnp.float32
BF16 = jnp.bfloat16

D_MODEL = 4096
HEAD_DIM = 128
MOBA_Q_HEADS = 32
MOBA_KV_HEADS = 2
MOBA_GROUP = 16
MOBA_BLOCK = 256
MOBA_TOPK = 3
MLA_HEADS = 32
MLA_Q_LORA = 1024
MLA_KV_LORA = 512
MLA_NOPE = 128
MLA_ROPE = 64
MLA_QK_DIM = MLA_NOPE + MLA_ROPE
MLA_V_DIM = 128
ROPE_THETA = 10000.0
CONV_W = 3
NORM_EPS = 1e-6
PAGE_SIZE = 128

NEG = -1e30
VMEM_LIMIT = 56 * 1024 * 1024
NT = (((1,), (1,)), ((), ()))


def _params(sem):
    return pltpu.CompilerParams(dimension_semantics=sem, vmem_limit_bytes=VMEM_LIMIT)


def _rmsnorm_kernel(x_ref, g_ref, o_ref):
    x = x_ref[...]
    y = x * lax.rsqrt(jnp.mean(x * x, axis=-1, keepdims=True) + NORM_EPS)
    o_ref[...] = (y * g_ref[...]).astype(o_ref.dtype)


def rmsnorm_rows(x, g, tr=256):
    m, d = x.shape
    return pl.pallas_call(
        _rmsnorm_kernel,
        grid=(m // tr,),
        in_specs=[pl.BlockSpec((tr, d), lambda i: (i, 0)),
                  pl.BlockSpec((1, d), lambda i: (0, 0))],
        out_specs=pl.BlockSpec((tr, d), lambda i: (i, 0)),
        out_shape=jax.ShapeDtypeStruct((m, d), BF16),
        compiler_params=_params(("parallel",)),
        name="rmsnorm",
    )(x, g.reshape(1, d))


def _mm_kernel(a_ref, w_ref, *rest, has_res):
    o_ref = rest[-1]
    acc = jnp.dot(a_ref[...].astype(BF16), w_ref[...].astype(BF16), preferred_element_type=F32)
    if has_res:
        acc = rest[0][...] + acc
    o_ref[...] = acc.astype(o_ref.dtype)


def matmul(a, w, res=None, *, tm, tn, out_dtype=F32, name="matmul", layer=None, a_buffers=2):
    m, k = a.shape
    n = w.shape[-1]
    assert m % tm == 0 and n % tn == 0, (m, n, tm, tn)
    a_spec = (pl.BlockSpec((tm, k), lambda i, j: (i, 0)) if a_buffers == 2 else
              pl.BlockSpec((tm, k), lambda i, j: (i, 0), pipeline_mode=pl.Buffered(a_buffers)))
    w_spec = (pl.BlockSpec((k, tn), lambda i, j: (0, j)) if layer is None else
              pl.BlockSpec((None, k, tn), lambda i, j: (layer, 0, j)))
    in_specs = [a_spec, w_spec]
    args = [a, w]
    if res is not None:
        in_specs.append(pl.BlockSpec((tm, tn), lambda i, j: (i, j)))
        args.append(res)
    return pl.pallas_call(
        functools.partial(_mm_kernel, has_res=res is not None),
        grid=(m // tm, n // tn),
        in_specs=in_specs,
        out_specs=pl.BlockSpec((tm, tn), lambda i, j: (i, j)),
        out_shape=jax.ShapeDtypeStruct((m, n), out_dtype),
        compiler_params=_params(("parallel", "parallel")),
        name=name,
    )(*args)


def _hmm_kernel(a_ref, w_ref, o_ref):
    o_ref[...] = jnp.dot(a_ref[...].astype(BF16), w_ref[...].astype(BF16),
                         preferred_element_type=F32).astype(o_ref.dtype)


def head_matmul(a, w, *, out_dtype, name):
    m = a.shape[0]
    nh, ka, n = w.shape
    return pl.pallas_call(
        _hmm_kernel,
        grid=(nh,),
        in_specs=[pl.BlockSpec((m, ka), lambda h: (0, h)),
                  pl.BlockSpec((None, ka, n), lambda h: (h, 0, 0))],
        out_specs=pl.BlockSpec((m, n), lambda h: (0, h)),
        out_shape=jax.ShapeDtypeStruct((m, nh * n), out_dtype),
        compiler_params=_params(("parallel",)),
        name=name,
    )(a, w)


def _silu_mul(gc, u):
    return (gc * (1.0 / (1.0 + jnp.exp(-gc)))) * u


def _ffn_up_prompt_kernel(a_ref, wg_ref, wu_ref, cw_ref, cb_ref, st_ref, act_ref, ns_ref):
    a = a_ref[...]
    g = jnp.dot(a, wg_ref[...].astype(BF16), preferred_element_type=F32)
    u = jnp.dot(a, wu_ref[...].astype(BF16), preferred_element_type=F32)
    tm = g.shape[0]
    row = lax.broadcasted_iota(jnp.int32, g.shape, 0)
    st0 = st_ref[0, 0:1, :]
    st1 = st_ref[0, 1:2, :]
    g1 = jnp.where(row >= 1, pltpu.roll(g, 1, 0), st1)
    g2 = jnp.where(row >= 2, pltpu.roll(g, 2, 0), jnp.where(row == 1, st1, st0))
    cw = cw_ref[...]
    gc = cb_ref[...] + cw[0:1, :] * g2
    gc = gc + cw[1:2, :] * g1
    gc = gc + cw[2:3, :] * g
    act_ref[...] = _silu_mul(gc, u).astype(act_ref.dtype)
    ns_ref[0] = g[tm - 2:tm, :]


def ffn_up_prompt(h, wg, wu, cw, cb, state, *, layer, seq, tn=256):
    m, k = h.shape
    n = wg.shape[2]
    nb = m // seq
    return pl.pallas_call(
        _ffn_up_prompt_kernel,
        grid=(nb, n // tn),
        in_specs=[pl.BlockSpec((seq, k), lambda i, j: (i, 0), pipeline_mode=pl.Buffered(1)),
                  pl.BlockSpec((None, k, tn), lambda i, j: (layer, 0, j)),
                  pl.BlockSpec((None, k, tn), lambda i, j: (layer, 0, j)),
                  pl.BlockSpec((CONV_W, tn), lambda i, j: (0, j)),
                  pl.BlockSpec((1, tn), lambda i, j: (0, j)),
                  pl.BlockSpec((1, 2, tn), lambda i, j: (i, 0, j))],
        out_specs=[pl.BlockSpec((seq, tn), lambda i, j: (i, j)),
                   pl.BlockSpec((1, 2, tn), lambda i, j: (i, 0, j))],
        out_shape=[jax.ShapeDtypeStruct((m, n), BF16),
                   jax.ShapeDtypeStruct((nb, 2, n), F32)],
        compiler_params=_params(("parallel", "parallel")),
        name="ffn_up_prompt",
    )(h, wg, wu, cw, cb.reshape(1, n), state)


def _ffn_up_sample_kernel(a_ref, wg_ref, wu_ref, cw_ref, cb_ref, st_ref, act_ref, ns_ref, *, nseq):
    a = a_ref[...]
    g = jnp.dot(a, wg_ref[...].astype(BF16), preferred_element_type=F32)
    u = jnp.dot(a, wu_ref[...].astype(BF16), preferred_element_type=F32)
    tm = g.shape[0]
    st0 = st_ref[0]
    st1 = st_ref[1]
    g1 = jnp.concatenate([st1, g[:tm - nseq]], axis=0)
    g2 = jnp.concatenate([st0, st1, g[:tm - 2 * nseq]], axis=0)
    cw = cw_ref[...]
    gc = cb_ref[...] + cw[0:1, :] * g2
    gc = gc + cw[1:2, :] * g1
    gc = gc + cw[2:3, :] * g
    act_ref[...] = _silu_mul(gc, u).astype(act_ref.dtype)
    ns_ref[0] = g[tm - 2 * nseq:tm - nseq]
    ns_ref[1] = g[tm - nseq:]


def ffn_up_sample(h, wg, wu, cw, cb, state_t, *, layer, nseq, tn=256):
    m, k = h.shape
    n = wg.shape[2]
    return pl.pallas_call(
        functools.partial(_ffn_up_sample_kernel, nseq=nseq),
        grid=(n // tn,),
        in_specs=[pl.BlockSpec((m, k), lambda j: (0, 0)),
                  pl.BlockSpec((None, k, tn), lambda j: (layer, 0, j)),
                  pl.BlockSpec((None, k, tn), lambda j: (layer, 0, j)),
                  pl.BlockSpec((CONV_W, tn), lambda j: (0, j)),
                  pl.BlockSpec((1, tn), lambda j: (0, j)),
                  pl.BlockSpec((2, nseq, tn), lambda j: (0, 0, j))],
        out_specs=[pl.BlockSpec((m, tn), lambda j: (0, j)),
                   pl.BlockSpec((2, nseq, tn), lambda j: (0, 0, j))],
        out_shape=[jax.ShapeDtypeStruct((m, n), BF16),
                   jax.ShapeDtypeStruct((2, nseq, n), F32)],
        compiler_params=_params(("parallel",)),
        name="ffn_up_sample",
    )(h, wg, wu, cw, cb.reshape(1, n), state_t)


def _moba_qknorm_kernel(x_ref, gq_ref, gk_ref, q_ref, k_ref, ka_ref, km_ref, *, nblk, nbp):
    gq = gq_ref[...]
    gk = gk_ref[...]
    for h in range(MOBA_Q_HEADS):
        x = x_ref[:, h * HEAD_DIM:(h + 1) * HEAD_DIM]
        y = x * lax.rsqrt(jnp.mean(x * x, axis=-1, keepdims=True) + NORM_EPS)
        q_ref[:, h * HEAD_DIM:(h + 1) * HEAD_DIM] = y * gq
    lane = lax.broadcasted_iota(jnp.int32, (x_ref.shape[0], HEAD_DIM), 1)
    offs = lax.broadcasted_iota(jnp.int32, (x_ref.shape[0], HEAD_DIM), 0).astype(F32)
    extra = jnp.where(lane == pl.program_id(0) % nblk, 1.0, 0.0)
    extra = jnp.where((lane >= nbp) & (lane < nbp + 3), offs, extra)
    extra = jnp.where((lane >= nbp + 3) & (lane < nbp + 6), 1.0, extra)
    onehot = extra.astype(ka_ref.dtype)
    for j in range(MOBA_KV_HEADS):
        c0 = (MOBA_Q_HEADS + j) * HEAD_DIM
        x = x_ref[:, c0:c0 + HEAD_DIM]
        y = (x * lax.rsqrt(jnp.mean(x * x, axis=-1, keepdims=True) + NORM_EPS)) * gk
        k_ref[:, j * HEAD_DIM:(j + 1) * HEAD_DIM] = y
        ka_ref[:, 2 * j * HEAD_DIM:(2 * j + 1) * HEAD_DIM] = y.astype(ka_ref.dtype)
        ka_ref[:, (2 * j + 1) * HEAD_DIM:(2 * j + 2) * HEAD_DIM] = onehot
        km_ref[0, :, j * HEAD_DIM:(j + 1) * HEAD_DIM] = jnp.mean(y, axis=0, keepdims=True)


def moba_qknorm(qkv, gq, gk, *, nblk):
    m, n = qkv.shape
    tr = MOBA_BLOCK
    nq = MOBA_Q_HEADS * HEAD_DIM
    nk = MOBA_KV_HEADS * HEAD_DIM
    nbp = nblk + (-nblk % 8)
    assert nbp + 6 <= HEAD_DIM and MOBA_BLOCK <= 256
    return pl.pallas_call(
        functools.partial(_moba_qknorm_kernel, nblk=nblk, nbp=nbp),
        grid=(m // tr,),
        in_specs=[pl.BlockSpec((tr, n), lambda i: (i, 0)),
                  pl.BlockSpec((1, HEAD_DIM), lambda i: (0, 0)),
                  pl.BlockSpec((1, HEAD_DIM), lambda i: (0, 0))],
        out_specs=[pl.BlockSpec((tr, nq), lambda i: (i, 0)),
                   pl.BlockSpec((tr, nk), lambda i: (i, 0)),
                   pl.BlockSpec((tr, 2 * nk), lambda i: (i, 0)),
                   pl.BlockSpec((1, 1, nk), lambda i: (i, 0, 0))],
        out_shape=[jax.ShapeDtypeStruct((m, nq), F32),
                   jax.ShapeDtypeStruct((m, nk), F32),
                   jax.ShapeDtypeStruct((m, 2 * nk), BF16),
                   jax.ShapeDtypeStruct((m // tr, 1, nk), F32)],
        compiler_params=_params(("parallel",)),
        name="moba_qknorm",
    )(qkv, gq.reshape(1, HEAD_DIM), gk.reshape(1, HEAD_DIM))


def _top3(gate, allowed, axis=1):
    nb = gate.shape[axis]
    pos = lax.broadcasted_iota(jnp.int32, gate.shape, axis)
    gm = jnp.where(allowed, gate, -jnp.inf)
    sel = jnp.zeros(gate.shape, F32)
    for _ in range(MOBA_TOPK):
        mx = jnp.max(gm, axis=axis, keepdims=True)
        idx = jnp.min(jnp.where(gm == mx, pos, nb), axis=axis, keepdims=True)
        pick = pos == idx
        sel = jnp.where(pick, 1.0, sel)
        gm = jnp.where(pick, -jnp.inf, gm)
    return jnp.where(allowed, sel, 0.0)


def _online_update(carry, s, ok, v):
    m, l, acc = carry
    s = jnp.where(ok, s, NEG)
    m_new = jnp.maximum(m, jnp.max(s, axis=1, keepdims=True))
    p = jnp.where(ok, jnp.exp(s - m_new), 0.0)
    alpha = jnp.exp(m - m_new)
    l = alpha * l + jnp.sum(p, axis=1, keepdims=True)
    acc = alpha * acc + jnp.dot(p.astype(BF16), v, preferred_element_type=F32)
    return m_new, l, acc


def _moba_prompt_kernel(slopes_ref, q_ref, ka_ref, vt_ref, km_ref, o_ref, *, hstack, nbp):
    kvh = pl.program_id(1)
    qt = pl.program_id(2)
    blk = MOBA_BLOCK
    cols = hstack * blk
    scale = HEAD_DIM ** -0.5
    km = km_ref[0, 0]
    bidx = lax.broadcasted_iota(jnp.int32, (nbp, cols), 0)
    cr = (lax.broadcasted_iota(jnp.int32, (blk, cols), 0)
          - lax.broadcasted_iota(jnp.int32, (blk, cols), 1) % blk)
    qoff = (lax.broadcasted_iota(jnp.int32, (1, cols), 1) % blk).astype(F32)
    arow = lax.broadcasted_iota(jnp.int32, (8, cols), 0)
    off_own = pl.multiple_of(qt * blk, blk)
    ka_own = ka_ref[pl.ds(off_own, blk), :]
    vt_own = vt_ref[0, 0, qt]
    pad = jnp.zeros((HEAD_DIM - nbp - 8, cols), F32)

    def split3(x):
        a = x.astype(BF16).astype(F32)
        b = (x - a).astype(BF16).astype(F32)
        return a, b, x - a - b

    for hg in range(MOBA_GROUP // hstack):
        heads = [hg * hstack + i for i in range(hstack)]
        qs = jnp.concatenate([q_ref[:, h * HEAD_DIM:(h + 1) * HEAD_DIM] for h in heads], axis=0)
        slope = jnp.concatenate([jnp.full((1, blk), slopes_ref[kvh * MOBA_GROUP + h], F32) for h in heads], axis=1)
        gate = lax.dot_general(km, qs, NT, precision=lax.Precision.HIGHEST, preferred_element_type=F32)
        sel = _top3(gate, bidx < qt, axis=0)
        blockbias = jnp.where((sel > 0.5) | (bidx == qt), 0.0, NEG)
        terms = split3(slope) + split3(-slope * qoff)
        alibi = jnp.zeros((8, cols), F32)
        for i, t in enumerate(terms):
            alibi = jnp.where(arow == i, t, alibi)
        qa = jnp.concatenate([(qs * scale).T, blockbias, alibi, pad], axis=0).astype(BF16)
        s = jnp.dot(ka_own, qa, preferred_element_type=F32)
        s = jnp.where(cr <= 0, s, NEG)
        m = jnp.max(s, axis=0, keepdims=True)
        p = jnp.exp(s - m)
        l = jnp.sum(p, axis=0, keepdims=True)
        acc = jnp.dot(vt_own, p.astype(BF16), preferred_element_type=F32)

        def body(n, carry, qa=qa, slope=slope):
            m, l, acc = carry
            off = pl.multiple_of(n * blk, blk)
            s = jnp.dot(ka_ref[pl.ds(off, blk), :], qa, preferred_element_type=F32)
            qterm = slope * (jnp.zeros((1, cols), jnp.int32) + (n - qt) * blk).astype(F32)
            m_new = jnp.maximum(m, jnp.max(s, axis=0, keepdims=True) + qterm)
            p = jnp.exp(s - (m_new - qterm))
            alpha = jnp.exp(m - m_new)
            l = alpha * l + jnp.sum(p, axis=0, keepdims=True)
            acc = alpha * acc + jnp.dot(vt_ref[0, 0, n], p.astype(BF16), preferred_element_type=F32)
            return m_new, l, acc

        m, l, acc = lax.fori_loop(0, qt, body, (m, l, acc))
        o = (acc / l).T
        for i, h in enumerate(heads):
            o_ref[:, h * HEAD_DIM:(h + 1) * HEAD_DIM] = o[i * blk:(i + 1) * blk].astype(o_ref.dtype)


def moba_prompt_attention(qn, ka, vt, kmean, slopes, *, nbatch, seq, hstack=16):
    m = qn.shape[0]
    nqt = seq // MOBA_BLOCK
    nbp = kmean.shape[2]
    assert nqt <= nbp <= HEAD_DIM and nbp % 8 == 0
    gw = MOBA_GROUP * HEAD_DIM
    return pl.pallas_call(
        functools.partial(_moba_prompt_kernel, hstack=hstack, nbp=nbp),
        grid=(nbatch, MOBA_KV_HEADS, nqt),
        in_specs=[pl.BlockSpec(memory_space=pltpu.SMEM),
                  pl.BlockSpec((MOBA_BLOCK, gw), lambda b, k, t: (b * nqt + t, k)),
                  pl.BlockSpec((seq, 2 * HEAD_DIM), lambda b, k, t: (b, k)),
                  pl.BlockSpec((1, 1, nqt, HEAD_DIM, MOBA_BLOCK), lambda b, k, t: (b, k, 0, 0, 0)),
                  pl.BlockSpec((1, 1, nbp, HEAD_DIM), lambda b, k, t: (b, k, 0, 0))],
        out_specs=pl.BlockSpec((MOBA_BLOCK, gw), lambda b, k, t: (b * nqt + t, k)),
        out_shape=jax.ShapeDtypeStruct((m, MOBA_Q_HEADS * HEAD_DIM), BF16),
        compiler_params=_params(("parallel", "parallel", "parallel")),
        name="moba_prompt_attn",
    )(slopes, qn, ka, vt, kmean)


def _kv_chunk(buf, slot, pps):
    pages = [jnp.concatenate([buf[slot, p, pl.ds(j, PAGE_SIZE, stride=MOBA_KV_HEADS), :]
                              for j in range(MOBA_KV_HEADS)], axis=1) for p in range(pps)]
    return jnp.concatenate(pages, axis=0)


def _moba_sample_kernel(pt_ref, ck_hbm, cv_hbm, q_ref, kn_ref, vn_ref, slope_ref, tpos_ref, o_ref,
                        kvbuf, ksem, vsem, s_ref, km_ref, sel_ref, m_ref, l_ref, acc_ref,
                        *, pps, nchunk, npages, past_len):
    c = pl.program_id(1)
    slot = _page_gather(pt_ref, [ck_hbm, cv_hbm], [kvbuf, kvbuf], [ksem, vsem], nsteps=2 * nchunk, npages=npages,
                        pps=pps, chunk_of=lambda st: [(0, st, st < nchunk), (1, st - nchunk, st >= nchunk)])
    scale = HEAD_DIM ** -0.5
    ck = pps * PAGE_SIZE
    bpc = ck // MOBA_BLOCK
    nblk = nchunk * bpc
    slope = slope_ref[...]
    tpos = tpos_ref[...]

    @pl.when(c < nchunk)
    def _scores():
        kc = _kv_chunk(kvbuf, slot, pps)
        s = lax.dot_general(q_ref[0].astype(BF16), kc.astype(BF16), NT, preferred_element_type=F32) * scale
        kpos = (c * ck + lax.broadcasted_iota(jnp.int32, (1, ck), 1)).astype(F32)
        s_ref[c] = s - slope * (tpos - kpos)
        km_ref[c] = jnp.sum(kc.reshape(bpc, MOBA_BLOCK, kc.shape[1]), axis=1) * (1.0 / MOBA_BLOCK)

    @pl.when(c == nchunk)
    def _select():
        km = km_ref[...].reshape(nblk, km_ref.shape[2])
        gate = lax.dot_general(q_ref[0], km, NT, precision=lax.Precision.HIGHEST,
                               preferred_element_type=F32)
        sel_ref[...] = _top3(gate, jnp.full(gate.shape, True))
        m_ref[...] = jnp.full(m_ref.shape, NEG, F32)
        l_ref[...] = jnp.zeros(l_ref.shape, F32)
        acc_ref[...] = jnp.zeros(acc_ref.shape, F32)

    @pl.when(c >= nchunk)
    def _attend():
        cc = c - nchunk
        s = s_ref[cc]
        brow = lax.broadcasted_iota(jnp.int32, (nblk, ck), 0)
        bcol = cc * bpc + lax.broadcasted_iota(jnp.int32, (nblk, ck), 1) // MOBA_BLOCK
        expand = jnp.where(brow == bcol, 1.0, 0.0).astype(BF16)
        ok = jnp.dot(sel_ref[...].astype(BF16), expand, preferred_element_type=F32) > 0.5
        vc = _kv_chunk(kvbuf, slot, pps).astype(BF16)
        m, l, acc = _online_update((m_ref[...], l_ref[...], acc_ref[...]), s, ok, vc)
        m_ref[...] = m
        l_ref[...] = l
        acc_ref[...] = acc

    @pl.when(c == 2 * nchunk - 1)
    def _own_block():
        kn = kn_ref[0]
        s = lax.dot_general(q_ref[0].astype(BF16), kn.astype(BF16), NT, preferred_element_type=F32) * scale
        j = lax.broadcasted_iota(jnp.int32, s.shape, 1).astype(F32)
        dist = tpos - (past_len + j)
        s = s - slope * dist
        m, l, acc = _online_update((m_ref[...], l_ref[...], acc_ref[...]), s, dist >= 0,
                                   vn_ref[0].astype(BF16))
        o_ref[0] = acc / l


def moba_sample_attention(page_table, cache_k, cache_v, q_bd, k_new, v_new, slope_rows, tpos_rows,
                          *, past_len, pps=32):
    ns, r, w = q_bd.shape
    ck = pps * PAGE_SIZE
    assert past_len % MOBA_BLOCK == 0 and ck % MOBA_BLOCK == 0
    npages = past_len // PAGE_SIZE
    nchunk = npages // pps
    nblk = past_len // MOBA_BLOCK
    prows = PAGE_SIZE * MOBA_KV_HEADS
    per_seq = lambda s, c, pt: (s, 0, 0)
    const2 = lambda s, c, pt: (0, 0)
    grid_spec = pltpu.PrefetchScalarGridSpec(
        num_scalar_prefetch=1,
        grid=(ns, 2 * nchunk),
        in_specs=[
            pl.BlockSpec(memory_space=pl.ANY),
            pl.BlockSpec(memory_space=pl.ANY),
            pl.BlockSpec((1, r, w), per_seq),
            pl.BlockSpec((1, 8, w), per_seq),
            pl.BlockSpec((1, 8, w), per_seq),
            pl.BlockSpec((r, 1), const2),
            pl.BlockSpec((r, 1), const2)],
        out_specs=pl.BlockSpec((1, r, w), per_seq),
        scratch_shapes=[pltpu.VMEM((2, pps, prows, HEAD_DIM), F32),
                        pltpu.SemaphoreType.DMA((2,)),
                        pltpu.SemaphoreType.DMA((2,)),
                        pltpu.VMEM((nchunk, r, ck), F32),
                        pltpu.VMEM((nchunk, ck // MOBA_BLOCK, w), F32),
                        pltpu.VMEM((r, nblk), F32),
                        pltpu.VMEM((r, 1), F32),
                        pltpu.VMEM((r, 1), F32),
                        pltpu.VMEM((r, w), F32)])
    return pl.pallas_call(
        functools.partial(_moba_sample_kernel, pps=pps, nchunk=nchunk, npages=npages, past_len=past_len),
        grid_spec=grid_spec,
        out_shape=jax.ShapeDtypeStruct((ns, r, w), F32),
        compiler_params=_params(("arbitrary", "arbitrary")),
        name="moba_sample_attn",
    )(page_table.reshape(-1), cache_k, cache_v, q_bd, k_new, v_new, slope_rows, tpos_rows)


def _mla_kv_prep_kernel(d_ref, gcq_ref, gckv_ref, gk_ref, gks_ref, cos_ref, sin_ref,
                        cq_ref, ckv_ref, ckvb_ref, kpe_ref):
    c0 = MLA_Q_LORA
    c1 = c0 + MLA_KV_LORA
    x = d_ref[:, 0:c0]
    y = x * lax.rsqrt(jnp.mean(x * x, axis=-1, keepdims=True) + NORM_EPS)
    cq_ref[...] = (y * gcq_ref[...]).astype(cq_ref.dtype)
    x = d_ref[:, c0:c1]
    y = (x * lax.rsqrt(jnp.mean(x * x, axis=-1, keepdims=True) + NORM_EPS)) * gckv_ref[...]
    ckv_ref[...] = y
    ckvb_ref[...] = y.astype(ckvb_ref.dtype)
    x = d_ref[:, c1:c1 + 128][:, :MLA_ROPE]
    xs = d_ref[:, c1 + 128:c1 + 256][:, :MLA_ROPE]
    rstd = lax.rsqrt(jnp.mean(x * x, axis=-1, keepdims=True) + NORM_EPS)
    kpe_ref[...] = ((x * rstd) * gk_ref[...]) * cos_ref[...] + ((xs * rstd) * gks_ref[...]) * sin_ref[...]


def mla_kv_prep(d, g_cq, g_ckv, g_kpe, g_kpe_sw, cos2, sin2s, *, tr=256):
    m, n = d.shape
    tab_rows = cos2.shape[0]
    ntab = tab_rows // tr if tab_rows >= tr else 1
    row = lambda i: (i, 0)
    const = lambda i: (0, 0)
    tab = lambda i: (i % ntab, 0)
    return pl.pallas_call(
        _mla_kv_prep_kernel,
        grid=(m // tr,),
        in_specs=[pl.BlockSpec((tr, n), row),
                  pl.BlockSpec((1, MLA_Q_LORA), const),
                  pl.BlockSpec((1, MLA_KV_LORA), const),
                  pl.BlockSpec((1, MLA_ROPE), const),
                  pl.BlockSpec((1, MLA_ROPE), const),
                  pl.BlockSpec((tr, MLA_ROPE), tab),
                  pl.BlockSpec((tr, MLA_ROPE), tab)],
        out_specs=[pl.BlockSpec((tr, MLA_Q_LORA), row),
                   pl.BlockSpec((tr, MLA_KV_LORA), row),
                   pl.BlockSpec((tr, MLA_KV_LORA), row),
                   pl.BlockSpec((tr, MLA_ROPE), row)],
        out_shape=[jax.ShapeDtypeStruct((m, MLA_Q_LORA), BF16),
                   jax.ShapeDtypeStruct((m, MLA_KV_LORA), F32),
                   jax.ShapeDtypeStruct((m, MLA_KV_LORA), BF16),
                   jax.ShapeDtypeStruct((m, MLA_ROPE), F32)],
        compiler_params=_params(("parallel",)),
        name="mla_kv_prep",
    )(d, g_cq.reshape(1, -1), g_ckv.reshape(1, -1), g_kpe.reshape(1, -1), g_kpe_sw.reshape(1, -1), cos2, sin2s)


def _mla_q_prep_kernel(x_ref, gn_ref, gr_ref, grs_ref, cos_ref, sin_ref, qn_ref, qpe_ref):
    scale = MLA_QK_DIM ** -0.5
    nn = MLA_HEADS * MLA_NOPE
    nr = MLA_HEADS * MLA_ROPE
    gn = gn_ref[...]
    gr = gr_ref[...]
    grs = grs_ref[...]
    cos = cos_ref[...]
    sin = sin_ref[...]
    lane = lax.broadcasted_iota(jnp.int32, (x_ref.shape[0], 128), 1)
    first = lane < MLA_ROPE
    for hp in range(MLA_HEADS // 2):
        r = x_ref[:, nn + hp * 128:nn + (hp + 1) * 128]
        rs = x_ref[:, nn + nr + hp * 128:nn + nr + (hp + 1) * 128]
        r2 = r * r
        ss_a = jnp.sum(jnp.where(first, r2, 0.0), axis=-1, keepdims=True)
        ss_b = jnp.sum(jnp.where(first, 0.0, r2), axis=-1, keepdims=True)
        rstd_pair = []
        for sub, ss_r in ((0, ss_a), (1, ss_b)):
            h = 2 * hp + sub
            n = x_ref[:, h * MLA_NOPE:(h + 1) * MLA_NOPE]
            ss = jnp.sum(n * n, axis=-1, keepdims=True) + ss_r
            rstd = lax.rsqrt(ss * (1.0 / MLA_QK_DIM) + NORM_EPS)
            qn_ref[:, h * MLA_NOPE:(h + 1) * MLA_NOPE] = (((n * rstd) * gn) * scale).astype(qn_ref.dtype)
            rstd_pair.append(rstd)
        rstd = jnp.where(first, rstd_pair[0], rstd_pair[1])
        qpe = ((r * rstd) * gr) * cos + ((rs * rstd) * grs) * sin
        qpe_ref[:, hp * 128:(hp + 1) * 128] = (qpe * scale).astype(qpe_ref.dtype)


def mla_q_prep(x, g_n, g_r2, g_rs2, cos4, sin4s, *, tr=256):
    m, n = x.shape
    tab_rows = cos4.shape[0]
    ntab = tab_rows // tr if tab_rows >= tr else 1
    row = lambda i: (i, 0)
    const = lambda i: (0, 0)
    tab = lambda i: (i % ntab, 0)
    return pl.pallas_call(
        _mla_q_prep_kernel,
        grid=(m // tr,),
        in_specs=[pl.BlockSpec((tr, n), row),
                  pl.BlockSpec((1, MLA_NOPE), const),
                  pl.BlockSpec((1, 128), const),
                  pl.BlockSpec((1, 128), const),
                  pl.BlockSpec((tr, 128), tab),
                  pl.BlockSpec((tr, 128), tab)],
        out_specs=[pl.BlockSpec((tr, MLA_HEADS * MLA_NOPE), row),
                   pl.BlockSpec((tr, MLA_HEADS * MLA_ROPE), row)],
        out_shape=[jax.ShapeDtypeStruct((m, MLA_HEADS * MLA_NOPE), BF16),
                   jax.ShapeDtypeStruct((m, MLA_HEADS * MLA_ROPE), BF16)],
        compiler_params=_params(("parallel",)),
        name="mla_q_prep",
    )(x, g_n.reshape(1, -1), g_r2.reshape(1, -1), g_rs2.reshape(1, -1), cos4, sin4s)


def _mla_prompt_kernel(qn_ref, qpe_ref, kn_ref, kpe_ref, v_ref, o_ref, *, tq):
    qt = pl.program_id(2)
    lane = lax.broadcasted_iota(jnp.int32, (tq, 128), 1)
    qpe = qpe_ref[...].astype(F32)
    halves = (lane < MLA_ROPE, lane >= MLA_ROPE)
    qs = [jnp.concatenate([qn_ref[:, i * MLA_NOPE:(i + 1) * MLA_NOPE],
                           jnp.where(halves[i], qpe, 0.0).astype(BF16)], axis=1) for i in range(2)]
    rc = (lax.broadcasted_iota(jnp.int32, (tq, tq), 0) - lax.broadcasted_iota(jnp.int32, (tq, tq), 1))

    def kv_block(n):
        off = pl.multiple_of(n * tq, tq)
        kpe = kpe_ref[pl.ds(off, tq), :]
        ks = [jnp.concatenate([kn_ref[pl.ds(off, tq), i * MLA_NOPE:(i + 1) * MLA_NOPE], kpe], axis=1)
              for i in range(2)]
        vs = [v_ref[pl.ds(off, tq), i * MLA_V_DIM:(i + 1) * MLA_V_DIM] for i in range(2)]
        return ks, vs

    ks, vs = kv_block(qt)
    carry = []
    for i in range(2):
        s = lax.dot_general(qs[i], ks[i], NT, preferred_element_type=F32)
        s = jnp.where(rc >= 0, s, NEG)
        m = jnp.max(s, axis=1, keepdims=True)
        p = jnp.exp(s - m)
        carry += [m, jnp.sum(p, axis=1, keepdims=True), jnp.dot(p.astype(BF16), vs[i], preferred_element_type=F32)]

    def body(n, carry):
        ks, vs = kv_block(n)
        out = []
        for i in range(2):
            m, l, acc = carry[3 * i:3 * i + 3]
            s = lax.dot_general(qs[i], ks[i], NT, preferred_element_type=F32)
            m_new = jnp.maximum(m, jnp.max(s, axis=1, keepdims=True))
            p = jnp.exp(s - m_new)
            alpha = jnp.exp(m - m_new)
            out += [m_new, alpha * l + jnp.sum(p, axis=1, keepdims=True),
                    alpha * acc + jnp.dot(p.astype(BF16), vs[i], preferred_element_type=F32)]
        return tuple(out)

    carry = lax.fori_loop(0, qt, body, tuple(carry))
    for i in range(2):
        m, l, acc = carry[3 * i:3 * i + 3]
        o_ref[:, i * MLA_V_DIM:(i + 1) * MLA_V_DIM] = (acc / l).astype(o_ref.dtype)


def mla_prompt_attention(qn, qpe, kv, kpe, *, nbatch, seq, tq=1024):
    m = qn.shape[0]
    nqt = seq // tq
    npair = MLA_HEADS // 2
    kpe = jnp.concatenate([kpe, kpe], axis=1).astype(BF16)
    return pl.pallas_call(
        functools.partial(_mla_prompt_kernel, tq=tq),
        grid=(nbatch, npair, nqt),
        in_specs=[pl.BlockSpec((tq, 2 * MLA_NOPE), lambda b, h, t: (b * nqt + t, h)),
                  pl.BlockSpec((tq, 2 * MLA_ROPE), lambda b, h, t: (b * nqt + t, h)),
                  pl.BlockSpec((seq, 2 * MLA_NOPE), lambda b, h, t: (b, h)),
                  pl.BlockSpec((seq, 2 * MLA_ROPE), lambda b, h, t: (b, 0)),
                  pl.BlockSpec((seq, 2 * MLA_V_DIM), lambda b, h, t: (b, npair + h))],
        out_specs=pl.BlockSpec((tq, 2 * MLA_V_DIM), lambda b, h, t: (b * nqt + t, h)),
        out_shape=jax.ShapeDtypeStruct((m, MLA_HEADS * MLA_V_DIM), BF16),
        compiler_params=_params(("parallel", "parallel", "parallel")),
        name="mla_prompt_attn",
    )(qn, qpe, kv, kpe, kv)


def _page_gather(pt_ref, srcs, bufs, sems, *, nsteps, npages, pps, chunk_of):
    seq, st = pl.program_id(0), pl.program_id(1)
    lin = seq * nsteps + st
    slot = jnp.bitwise_and(lin, 1)

    def copies(which, page, slot_, j):
        idx = list(range(len(srcs))) if which is None else [which]
        return [pltpu.make_async_copy(srcs[i].at[page], bufs[i].at[slot_, j], sems[i].at[slot_]) for i in idx]

    def issue(seq_, st_, slot_):
        for which, chunk, pred in chunk_of(st_):
            @pl.when(pred)
            def _(which=which, chunk=chunk):
                base = seq_ * npages + chunk * pps

                def body(j, carry):
                    for cp in copies(which, pt_ref[base + j], slot_, j):
                        cp.start()
                    return carry

                lax.fori_loop(0, pps, body, 0)

    @pl.when(lin == 0)
    def _first():
        issue(seq, st, slot)

    wrap = st + 1 == nsteps
    nseq, nst = jnp.where(wrap, seq + 1, seq), jnp.where(wrap, 0, st + 1)

    @pl.when(nseq < pl.num_programs(0))
    def _next():
        issue(nseq, nst, 1 - slot)

    for which, _, pred in chunk_of(st):
        @pl.when(pred)
        def _(which=which):
            def body(j, carry):
                for cp in copies(which, 0, slot, j):
                    cp.wait()
                return carry

            lax.fori_loop(0, pps, body, 0)

    return slot


def _mla_sample_kernel(pt_ref, ckv_hbm, kpe_hbm, ql_ref, qp_ref, cn_ref, rn_ref, tpos_ref, o_ref,
                       cbuf, rbuf, csem, rsem, m_ref, l_ref, acc_ref, *, pps, nchunk, npages, past_len, nsub):
    c = pl.program_id(1)
    slot = _page_gather(pt_ref, [ckv_hbm, kpe_hbm], [cbuf, rbuf], [csem, rsem], nsteps=nchunk, npages=npages,
                        pps=pps, chunk_of=lambda st: [(None, st, st >= 0)])
    ql = ql_ref[0]
    qp = qp_ref[0]

    @pl.when(c == 0)
    def _init():
        m_ref[...] = jnp.full(m_ref.shape, NEG, F32)
        l_ref[...] = jnp.zeros(l_ref.shape, F32)
        acc_ref[...] = jnp.zeros(acc_ref.shape, F32)

    m, l, acc = m_ref[...], l_ref[...], acc_ref[...]
    per = pps // nsub
    ccs, ss = [], []
    for sub in range(nsub):
        cc = cbuf[slot, sub * per:(sub + 1) * per].reshape(per * PAGE_SIZE, MLA_KV_LORA).astype(BF16)
        rr = jnp.concatenate([rbuf[slot, j] for j in range(sub * per, (sub + 1) * per)], axis=1).astype(BF16)
        ccs.append(cc)
        ss.append(lax.dot_general(ql, cc, NT, preferred_element_type=F32)
                  + jnp.dot(qp, rr, preferred_element_type=F32))
    for s, cc in zip(ss, ccs):
        m_new = jnp.maximum(m, jnp.max(s, axis=1, keepdims=True))
        p = jnp.exp(s - m_new)
        alpha = jnp.exp(m - m_new)
        l = alpha * l + jnp.sum(p, axis=1, keepdims=True)
        acc = alpha * acc + jnp.dot(p.astype(BF16), cc, preferred_element_type=F32)
        m = m_new
    m_ref[...] = m
    l_ref[...] = l
    acc_ref[...] = acc

    @pl.when(c == nchunk - 1)
    def _new_rows():
        cn = cn_ref[0].astype(BF16)
        rn = rn_ref[0].astype(BF16)
        s2 = (lax.dot_general(ql, cn, NT, preferred_element_type=F32)
              + lax.dot_general(qp, rn, NT, preferred_element_type=F32))
        j = lax.broadcasted_iota(jnp.int32, s2.shape, 1).astype(F32)
        ok = (past_len + j) <= tpos_ref[...]
        m2, l2, acc2 = _online_update((m_ref[...], l_ref[...], acc_ref[...]), s2, ok, cn)
        o_ref[0] = (acc2 / l2).astype(o_ref.dtype)


def mla_sample_attention(page_table, cache_ckv, cache_kpe, q_lat, q_pe, c_new, r_new, tpos_rows,
                         *, past_len, pps=32, nsub=4):
    ns, r, _ = q_lat.shape
    npages = past_len // PAGE_SIZE
    nchunk = npages // pps
    per_seq = lambda s, c, pt: (s, 0, 0)
    grid_spec = pltpu.PrefetchScalarGridSpec(
        num_scalar_prefetch=1,
        grid=(ns, nchunk),
        in_specs=[
            pl.BlockSpec(memory_space=pl.ANY),
            pl.BlockSpec(memory_space=pl.ANY),
            pl.BlockSpec((1, r, MLA_KV_LORA), per_seq),
            pl.BlockSpec((1, r, MLA_ROPE), per_seq),
            pl.BlockSpec((1, 8, MLA_KV_LORA), per_seq),
            pl.BlockSpec((1, 8, MLA_ROPE), per_seq),
            pl.BlockSpec((r, 1), lambda s, c, pt: (0, 0))],
        out_specs=pl.BlockSpec((1, r, MLA_KV_LORA), per_seq),
        scratch_shapes=[pltpu.VMEM((2, pps, PAGE_SIZE, MLA_KV_LORA), F32),
                        pltpu.VMEM((2, pps, MLA_ROPE, PAGE_SIZE), F32),
                        pltpu.SemaphoreType.DMA((2,)),
                        pltpu.SemaphoreType.DMA((2,)),
                        pltpu.VMEM((r, 1), F32), pltpu.VMEM((r, 1), F32), pltpu.VMEM((r, MLA_KV_LORA), F32)])
    return pl.pallas_call(
        functools.partial(_mla_sample_kernel, pps=pps, nchunk=nchunk, npages=npages, past_len=past_len, nsub=nsub),
        grid_spec=grid_spec,
        out_shape=jax.ShapeDtypeStruct((ns, r, MLA_KV_LORA), BF16),
        compiler_params=_params(("arbitrary", "arbitrary")),
        name="mla_sample_attn",
    )(page_table.reshape(-1), cache_ckv, cache_kpe, q_lat, q_pe, c_new, r_new, tpos_rows)


def _rope_tables(pos, reps):
    half = MLA_ROPE // 2
    inv = ROPE_THETA ** (-jnp.arange(half, dtype=F32) / half)
    ang = pos.astype(F32)[:, None] * inv[None, :]
    cos, sin = jnp.cos(ang), jnp.sin(ang)
    cos2 = jnp.concatenate([cos, cos], axis=1)
    sin2 = jnp.concatenate([-sin, sin], axis=1)
    return jnp.tile(cos2, (1, reps)), jnp.tile(sin2, (1, reps))


def _swap_halves(x):
    half = x.shape[-1] // 2
    return jnp.concatenate([x[..., half:], x[..., :half]], axis=-1)


def _pad_rows8(x, ns):
    t = x.shape[0] // ns
    y = x.reshape(t, ns, x.shape[1]).transpose(1, 0, 2)
    return jnp.pad(y, ((0, 0), (0, 8 - t), (0, 0)))


def _moba_layer(hp, hs, xp, xs, cache_k, cache_v, page_table, w_qkv, g_q, g_k, w_o, *, nbatch, seq, ns, t, past_len):
    nq = MOBA_Q_HEADS * HEAD_DIM
    nk = MOBA_KV_HEADS * HEAD_DIM
    slopes = jnp.exp2(-8.0 * jnp.arange(1, MOBA_Q_HEADS + 1, dtype=F32) / MOBA_Q_HEADS)
    qkv_p = matmul(hp, w_qkv, tm=1024, tn=512, name="moba_qkv_p")
    nqt = seq // MOBA_BLOCK
    qn_p, kn_p, ka_p, km = moba_qknorm(qkv_p, g_q, g_k, nblk=nqt)
    kmean = km.reshape(nbatch, nqt, MOBA_KV_HEADS, HEAD_DIM).transpose(0, 2, 1, 3)
    kmean = jnp.pad(kmean, ((0, 0), (0, 0), (0, -nqt % 8), (0, 0)))
    vt_p = qkv_p[:, nq + nk:].astype(BF16).reshape(nbatch, nqt, MOBA_BLOCK, MOBA_KV_HEADS, HEAD_DIM)
    vt_p = vt_p.transpose(0, 3, 1, 4, 2)
    o_p = moba_prompt_attention(qn_p, ka_p, vt_p, kmean, slopes, nbatch=nbatch, seq=seq)
    xp = matmul(o_p, w_o, xp, tm=1024, tn=512, name="moba_wo_p")
    qkv_s = matmul(hs, w_qkv, tm=hs.shape[0], tn=512, name="moba_qkv_s")
    qn_s, kn_s, _, _ = moba_qknorm(qkv_s, g_q, g_k, nblk=1)
    vn_s = qkv_s[:, nq + nk:]
    r = t * MOBA_GROUP
    q5 = qn_s.reshape(t, ns, MOBA_KV_HEADS, MOBA_GROUP, HEAD_DIM).transpose(1, 2, 0, 3, 4)
    q5 = q5.reshape(ns, MOBA_KV_HEADS, r, HEAD_DIM)
    zeros = jnp.zeros_like(q5[:, 0])
    q_bd = jnp.concatenate([jnp.concatenate([q5[:, 0], zeros], axis=-1),
                            jnp.concatenate([zeros, q5[:, 1]], axis=-1)], axis=1)
    rows = np.arange(MOBA_KV_HEADS * r)
    row_kvh, row_t, row_g = rows // r, (rows % r) // MOBA_GROUP, rows % MOBA_GROUP
    slope_rows = slopes[row_kvh * MOBA_GROUP + row_g].reshape(-1, 1)
    tpos_rows = jnp.asarray((past_len + row_t).astype(np.float32).reshape(-1, 1))
    o_bd = moba_sample_attention(page_table, cache_k, cache_v, q_bd, _pad_rows8(kn_s, ns), _pad_rows8(vn_s, ns),
                                 slope_rows, tpos_rows, past_len=past_len)
    o5 = jnp.stack([o_bd[:, :r, :HEAD_DIM], o_bd[:, r:, HEAD_DIM:]], axis=1)
    o_s = o5.reshape(ns, MOBA_KV_HEADS, t, MOBA_GROUP, HEAD_DIM).transpose(2, 0, 1, 3, 4)
    o_s = o_s.reshape(t * ns, nq).astype(BF16)
    xs = matmul(o_s, w_o, xs, tm=o_s.shape[0], tn=512, name="moba_wo_s")
    return xp, xs, kn_p, qkv_p[:, nq + nk:], kn_s, vn_s


def _mla_layer(hp, hs, xp, xs, cache_ckv, cache_kpe, page_table, w_down, g_cq, w_uq, g_q, g_ckv, g_kpe,
               w_uk, w_uv, w_o, *, nbatch, seq, ns, t, past_len):
    h = MLA_HEADS
    c1 = MLA_Q_LORA + MLA_KV_LORA
    w_kpe = w_down[:, c1:]
    zpad = jnp.zeros((w_down.shape[0], 128 - MLA_ROPE), w_down.dtype)
    w_down_x = jnp.concatenate([w_down[:, :c1], w_kpe, zpad, _swap_halves(w_kpe), zpad], axis=1)
    w_uq3 = w_uq.reshape(MLA_Q_LORA, h, MLA_QK_DIM)
    w_rope = w_uq3[:, :, MLA_NOPE:]
    w_uq_x = jnp.concatenate([w_uq3[:, :, :MLA_NOPE].reshape(MLA_Q_LORA, -1),
                              w_rope.reshape(MLA_Q_LORA, -1),
                              _swap_halves(w_rope).reshape(MLA_Q_LORA, -1)], axis=1)
    w_kv = jnp.concatenate([w_uk.reshape(MLA_KV_LORA, -1), w_uv.reshape(MLA_KV_LORA, -1)], axis=1)
    g_n, g_r = g_q[:MLA_NOPE], g_q[MLA_NOPE:]
    g_r2, g_rs2 = jnp.tile(g_r, 2), jnp.tile(_swap_halves(g_r), 2)
    pos_p = jnp.arange(seq, dtype=jnp.int32)
    pos_s = jnp.repeat(past_len + jnp.arange(t, dtype=jnp.int32), ns)

    def project(hx, pos, tm):
        cos2, sin2 = _rope_tables(pos, 1)
        cos4, sin4 = _rope_tables(pos, 2)
        d = matmul(hx, w_down_x, tm=tm, tn=256, name="mla_down")
        cq, ckv, ckv_b, kpe = mla_kv_prep(d, g_cq, g_ckv, g_kpe, _swap_halves(g_kpe), cos2, sin2)
        qx = matmul(cq, w_uq_x, tm=tm, tn=512, name="mla_uq")
        qn, qpe = mla_q_prep(qx, g_n, g_r2, g_rs2, cos4, sin4)
        return qn, qpe, ckv, ckv_b, kpe

    qn_p, qpe_p, ckv_p, ckvb_p, kpe_p = project(hp, pos_p, 1024)
    kv_p = matmul(ckvb_p, w_kv, tm=1024, tn=512, out_dtype=BF16, name="mla_kv_up")
    o_p = mla_prompt_attention(qn_p, qpe_p, kv_p, kpe_p, nbatch=nbatch, seq=seq)
    xp = matmul(o_p, w_o, xp, tm=1024, tn=512, name="mla_wo_p")
    m_s = hs.shape[0]
    qn_s, qpe_s, ckv_s, _, kpe_s = project(hs, pos_s, m_s)
    w_uk_t = w_uk.transpose(1, 2, 0)
    q_lat = head_matmul(qn_s, w_uk_t, out_dtype=BF16, name="mla_q_lat")
    r = t * h
    q_lat = q_lat.reshape(t, ns, h, MLA_KV_LORA).transpose(1, 0, 2, 3).reshape(ns, r, MLA_KV_LORA)
    q_pe = qpe_s.reshape(t, ns, h, MLA_ROPE).transpose(1, 0, 2, 3).reshape(ns, r, MLA_ROPE)
    tpos_rows = jnp.asarray((past_len + np.arange(r) // h).astype(np.float32).reshape(-1, 1))
    o_lat = mla_sample_attention(page_table, cache_ckv, cache_kpe, q_lat, q_pe,
                                 _pad_rows8(ckv_s, ns), _pad_rows8(kpe_s, ns), tpos_rows, past_len=past_len)
    o_lat = o_lat.reshape(ns, t, h, MLA_KV_LORA).transpose(1, 0, 2, 3).reshape(t * ns, h * MLA_KV_LORA)
    o_s = head_matmul(o_lat, w_uv.transpose(1, 0, 2), out_dtype=BF16, name="mla_o_up")
    xs = matmul(o_s, w_o, xs, tm=m_s, tn=512, name="mla_wo_s")
    return xp, xs, ckv_p, kpe_p, ckv_s, kpe_s


def _conv_ffn(xp, xs, g_norm, state_s, w_gate, w_up, conv_w, conv_b, w_down, *, layer, nbatch, seq, ns):
    hp = rmsnorm_rows(xp, g_norm)
    hs = rmsnorm_rows(xs, g_norm)
    zero_state = jnp.zeros((nbatch, CONV_W - 1, w_gate.shape[2]), F32)
    act_p, st_p = ffn_up_prompt(hp, w_gate, w_up, conv_w, conv_b, zero_state, layer=layer, seq=seq)
    xp = matmul(act_p, w_down, xp, tm=1024, tn=256, name="ffn_down_p", layer=layer, a_buffers=1)
    act_s, st_s = ffn_up_sample(hs, w_gate, w_up, conv_w, conv_b, state_s.transpose(1, 0, 2), layer=layer, nseq=ns)
    xs = matmul(act_s, w_down, xs, tm=xs.shape[0], tn=256, name="ffn_down_s", layer=layer)
    return xp, xs, st_p, st_s.transpose(1, 0, 2)


def kernel(x_prompt, x_sample, cache_moba_k, cache_moba_v, cache_mla_ckv, cache_mla_kpe, state_ffn_conv, page_table, g_mix_norm, g_ffn_norm, moba_w_qkv, moba_g_q, moba_g_k, moba_w_o, mla_w_down, mla_g_cq, mla_w_uq, mla_g_q, mla_g_ckv, mla_g_kpe, mla_w_uk, mla_w_uv, mla_w_o, ffn_w_gate, ffn_w_up, ffn_conv_w, ffn_conv_b, ffn_w_down):
    nbatch, seq, d = x_prompt.shape
    ns, t, _ = x_sample.shape
    n_pool = cache_moba_k.shape[1]
    past_len = page_table.shape[1] * PAGE_SIZE
    depth = g_mix_norm.shape[0]
    dims = dict(nbatch=nbatch, seq=seq, ns=ns, t=t, past_len=past_len)
    xp = x_prompt.reshape(nbatch * seq, d)
    xs = x_sample.transpose(1, 0, 2).reshape(t * ns, d)
    outs = {k: [] for k in ("mk_p", "mv_p", "mk_s", "mv_s", "mc_p", "mr_p", "mc_s", "mr_s", "cv_p", "cv_s")}

    def seq_major(x):
        return x.reshape(t, ns, -1).transpose(1, 0, 2)

    for i in range(depth):
        hp = rmsnorm_rows(xp, g_mix_norm[i])
        hs = rmsnorm_rows(xs, g_mix_norm[i])
        j = i // 2
        if i % 2 == 0:
            ck = cache_moba_k[j].reshape(n_pool, PAGE_SIZE * MOBA_KV_HEADS, HEAD_DIM)
            cv = cache_moba_v[j].reshape(n_pool, PAGE_SIZE * MOBA_KV_HEADS, HEAD_DIM)
            xp, xs, kp, vp, ks, vs = _moba_layer(hp, hs, xp, xs, ck, cv, page_table, moba_w_qkv[j], moba_g_q[j],
                                                 moba_g_k[j], moba_w_o[j], **dims)
            outs["mk_p"].append(kp.reshape(nbatch, seq, MOBA_KV_HEADS, HEAD_DIM))
            outs["mv_p"].append(vp.reshape(nbatch, seq, MOBA_KV_HEADS, HEAD_DIM))
            outs["mk_s"].append(seq_major(ks).reshape(ns, t, MOBA_KV_HEADS, HEAD_DIM))
            outs["mv_s"].append(seq_major(vs).reshape(ns, t, MOBA_KV_HEADS, HEAD_DIM))
        else:
            xp, xs, cp, rp, cs, rs = _mla_layer(hp, hs, xp, xs, cache_mla_ckv[j],
                                                jnp.swapaxes(cache_mla_kpe[j], 1, 2), page_table,
                                                mla_w_down[j], mla_g_cq[j], mla_w_uq[j], mla_g_q[j], mla_g_ckv[j],
                                                mla_g_kpe[j], mla_w_uk[j], mla_w_uv[j], mla_w_o[j], **dims)
            outs["mc_p"].append(cp.reshape(nbatch, seq, -1))
            outs["mr_p"].append(rp.reshape(nbatch, seq, -1))
            outs["mc_s"].append(seq_major(cs))
            outs["mr_s"].append(seq_major(rs))
        xp, xs, st_p, st_s = _conv_ffn(xp, xs, g_ffn_norm[i], state_ffn_conv[i], ffn_w_gate, ffn_w_up,
                                       ffn_conv_w[i], ffn_conv_b[i], ffn_w_down, layer=i, nbatch=nbatch, seq=seq,
                                       ns=ns)
        outs["cv_p"].append(st_p)
        outs["cv_s"].append(st_s)
    y_p = xp.reshape(nbatch, seq, d)
    y_s = seq_major(xs)
    return (y_p, y_s, jnp.stack(outs["mk_p"]), jnp.stack(outs["mv_p"]), jnp.stack(outs["mk_s"]),
            jnp.stack(outs["mv_s"]), jnp.stack(outs["mc_p"]), jnp.stack(outs["mr_p"]), jnp.stack(outs["mc_s"]),
            jnp.stack(outs["mr_s"]), jnp.stack(outs["cv_p"]), jnp.stack(outs["cv_s"]))
```

```python
import functools

import numpy as np
import jax
import jax.numpy as jnp
from jax import lax
from jax.experimental import pallas as pl
from jax.experimental.pallas import tpu as pltpu

F32 = jnp.float32
BF16 = jnp.bfloat16

D_MODEL = 4096
HEAD_DIM = 128
MOBA_Q_HEADS = 32
MOBA_KV_HEADS = 2
MOBA_GROUP = 16
MOBA_BLOCK = 256
MOBA_TOPK = 3
MLA_HEADS = 32
MLA_Q_LORA = 1024
MLA_KV_LORA = 512
MLA_NOPE = 128
MLA_ROPE = 64
MLA_QK_DIM = MLA_NOPE + MLA_ROPE
MLA_V_DIM = 128
ROPE_THETA = 10000.0
CONV_W = 3
NORM_EPS = 1e-6
PAGE_SIZE = 128

NEG = -1e30
VMEM_LIMIT = 56 * 1024 * 1024
NT = (((1,), (1,)), ((), ()))


def _params(sem):
    return pltpu.CompilerParams(dimension_semantics=sem, vmem_limit_bytes=VMEM_LIMIT)


def _rmsnorm_kernel(x_ref, g_ref, o_ref):
    x = x_ref[...]
    y = x * lax.rsqrt(jnp.mean(x * x, axis=-1, keepdims=True) + NORM_EPS)
    o_ref[...] = (y * g_ref[...]).astype(o_ref.dtype)


def rmsnorm_rows(x, g, tr=256):
    m, d = x.shape
    return pl.pallas_call(
        _rmsnorm_kernel,
        grid=(m // tr,),
        in_specs=[pl.BlockSpec((tr, d), lambda i: (i, 0)),
                  pl.BlockSpec((1, d), lambda i: (0, 0))],
        out_specs=pl.BlockSpec((tr, d), lambda i: (i, 0)),
        out_shape=jax.ShapeDtypeStruct((m, d), BF16),
        compiler_params=_params(("parallel",)),
        name="rmsnorm",
    )(x, g.reshape(1, d))


def _mm_kernel(a_ref, w_ref, *rest, has_res):
    o_ref = rest[-1]
    acc = jnp.dot(a_ref[...].astype(BF16), w_ref[...].astype(BF16), preferred_element_type=F32)
    if has_res:
        acc = rest[0][...] + acc
    o_ref[...] = acc.astype(o_ref.dtype)


def matmul(a, w, res=None, *, tm, tn, out_dtype=F32, name="matmul", layer=None, a_buffers=2):
    m, k = a.shape
    n = w.shape[-1]
    assert m % tm == 0 and n % tn == 0, (m, n, tm, tn)
    a_spec = (pl.BlockSpec((tm, k), lambda i, j: (i, 0)) if a_buffers == 2 else
              pl.BlockSpec((tm, k), lambda i, j: (i, 0), pipeline_mode=pl.Buffered(a_buffers)))
    w_spec = (pl.BlockSpec((k, tn), lambda i, j: (0, j)) if layer is None else
              pl.BlockSpec((None, k, tn), lambda i, j: (layer, 0, j)))
    in_specs = [a_spec, w_spec]
    args = [a, w]
    if res is not None:
        in_specs.append(pl.BlockSpec((tm, tn), lambda i, j: (i, j)))
        args.append(res)
    return pl.pallas_call(
        functools.partial(_mm_kernel, has_res=res is not None),
        grid=(m // tm, n // tn),
        in_specs=in_specs,
        out_specs=pl.BlockSpec((tm, tn), lambda i, j: (i, j)),
        out_shape=jax.ShapeDtypeStruct((m, n), out_dtype),
        compiler_params=_params(("parallel", "parallel")),
        name=name,
    )(*args)


def _hmm_kernel(a_ref, w_ref, o_ref):
    o_ref[...] = jnp.dot(a_ref[...].astype(BF16), w_ref[...].astype(BF16),
                         preferred_element_type=F32).astype(o_ref.dtype)


def head_matmul(a, w, *, out_dtype, name):
    m = a.shape[0]
    nh, ka, n = w.shape
    return pl.pallas_call(
        _hmm_kernel,
        grid=(nh,),
        in_specs=[pl.BlockSpec((m, ka), lambda h: (0, h)),
                  pl.BlockSpec((None, ka, n), lambda h: (h, 0, 0))],
        out_specs=pl.BlockSpec((m, n), lambda h: (0, h)),
        out_shape=jax.ShapeDtypeStruct((m, nh * n), out_dtype),
        compiler_params=_params(("parallel",)),
        name=name,
    )(a, w)


def _silu_mul(gc, u):
    return (gc * (1.0 / (1.0 + jnp.exp(-gc)))) * u


def _ffn_up_prompt_kernel(a_ref, wg_ref, wu_ref, cw_ref, cb_ref, st_ref, act_ref, ns_ref):
    a = a_ref[...]
    g = jnp.dot(a, wg_ref[...].astype(BF16), preferred_element_type=F32)
    u = jnp.dot(a, wu_ref[...].astype(BF16), preferred_element_type=F32)
    tm = g.shape[0]
    row = lax.broadcasted_iota(jnp.int32, g.shape, 0)
    st0 = st_ref[0, 0:1, :]
    st1 = st_ref[0, 1:2, :]
    g1 = jnp.where(row >= 1, pltpu.roll(g, 1, 0), st1)
    g2 = jnp.where(row >= 2, pltpu.roll(g, 2, 0), jnp.where(row == 1, st1, st0))
    cw = cw_ref[...]
    gc = cb_ref[...] + cw[0:1, :] * g2
    gc = gc + cw[1:2, :] * g1
    gc = gc + cw[2:3, :] * g
    act_ref[...] = _silu_mul(gc, u).astype(act_ref.dtype)
    ns_ref[0] = g[tm - 2:tm, :]


def ffn_up_prompt(h, wg, wu, cw, cb, state, *, layer, seq, tn=256):
    m, k = h.shape
    n = wg.shape[2]
    nb = m // seq
    return pl.pallas_call(
        _ffn_up_prompt_kernel,
        grid=(nb, n // tn),
        in_specs=[pl.BlockSpec((seq, k), lambda i, j: (i, 0), pipeline_mode=pl.Buffered(1)),
                  pl.BlockSpec((None, k, tn), lambda i, j: (layer, 0, j)),
                  pl.BlockSpec((None, k, tn), lambda i, j: (layer, 0, j)),
                  pl.BlockSpec((CONV_W, tn), lambda i, j: (0, j)),
                  pl.BlockSpec((1, tn), lambda i, j: (0, j)),
                  pl.BlockSpec((1, 2, tn), lambda i, j: (i, 0, j))],
        out_specs=[pl.BlockSpec((seq, tn), lambda i, j: (i, j)),
                   pl.BlockSpec((1, 2, tn), lambda i, j: (i, 0, j))],
        out_shape=[jax.ShapeDtypeStruct((m, n), BF16),
                   jax.ShapeDtypeStruct((nb, 2, n), F32)],
        compiler_params=_params(("parallel", "parallel")),
        name="ffn_up_prompt",
    )(h, wg, wu, cw, cb.reshape(1, n), state)


def _ffn_up_sample_kernel(a_ref, wg_ref, wu_ref, cw_ref, cb_ref, st_ref, act_ref, ns_ref, *, nseq):
    a = a_ref[...]
    g = jnp.dot(a, wg_ref[...].astype(BF16), preferred_element_type=F32)
    u = jnp.dot(a, wu_ref[...].astype(BF16), preferred_element_type=F32)
    tm = g.shape[0]
    st0 = st_ref[0]
    st1 = st_ref[1]
    g1 = jnp.concatenate([st1, g[:tm - nseq]], axis=0)
    g2 = jnp.concatenate([st0, st1, g[:tm - 2 * nseq]], axis=0)
    cw = cw_ref[...]
    gc = cb_ref[...] + cw[0:1, :] * g2
    gc = gc + cw[1:2, :] * g1
    gc = gc + cw[2:3, :] * g
    act_ref[...] = _silu_mul(gc, u).astype(act_ref.dtype)
    ns_ref[0] = g[tm - 2 * nseq:tm - nseq]
    ns_ref[1] = g[tm - nseq:]


def ffn_up_sample(h, wg, wu, cw, cb, state_t, *, layer, nseq, tn=256):
    m, k = h.shape
    n = wg.shape[2]
    return pl.pallas_call(
        functools.partial(_ffn_up_sample_kernel, nseq=nseq),
        grid=(n // tn,),
        in_specs=[pl.BlockSpec((m, k), lambda j: (0, 0)),
                  pl.BlockSpec((None, k, tn), lambda j: (layer, 0, j)),
                  pl.BlockSpec((None, k, tn), lambda j: (layer, 0, j)),
                  pl.BlockSpec((CONV_W, tn), lambda j: (0, j)),
                  pl.BlockSpec((1, tn), lambda j: (0, j)),
                  pl.BlockSpec((2, nseq, tn), lambda j: (0, 0, j))],
        out_specs=[pl.BlockSpec((m, tn), lambda j: (0, j)),
                   pl.BlockSpec((2, nseq, tn), lambda j: (0, 0, j))],
        out_shape=[jax.ShapeDtypeStruct((m, n), BF16),
                   jax.ShapeDtypeStruct((2, nseq, n), F32)],
        compiler_params=_params(("parallel",)),
        name="ffn_up_sample",
    )(h, wg, wu, cw, cb.reshape(1, n), state_t)


def _moba_qknorm_kernel(x_ref, gq_ref, gk_ref, q_ref, k_ref, ka_ref, km_ref, *, nblk, nbp):
    gq = gq_ref[...]
    gk = gk_ref[...]
    for h in range(MOBA_Q_HEADS):
        x = x_ref[:, h * HEAD_DIM:(h + 1) * HEAD_DIM]
        y = x * lax.rsqrt(jnp.mean(x * x, axis=-1, keepdims=True) + NORM_EPS)
        q_ref[:, h * HEAD_DIM:(h + 1) * HEAD_DIM] = y * gq
    lane = lax.broadcasted_iota(jnp.int32, (x_ref.shape[0], HEAD_DIM), 1)
    offs = lax.broadcasted_iota(jnp.int32, (x_ref.shape[0], HEAD_DIM), 0).astype(F32)
    extra = jnp.where(lane == pl.program_id(0) % nblk, 1.0, 0.0)
    extra = jnp.where((lane >= nbp) & (lane < nbp + 3), offs, extra)
    extra = jnp.where((lane >= nbp + 3) & (lane < nbp + 6), 1.0, extra)
    onehot = extra.astype(ka_ref.dtype)
    for j in range(MOBA_KV_HEADS):
        c0 = (MOBA_Q_HEADS + j) * HEAD_DIM
        x = x_ref[:, c0:c0 + HEAD_DIM]
        y = (x * lax.rsqrt(jnp.mean(x * x, axis=-1, keepdims=True) + NORM_EPS)) * gk
        k_ref[:, j * HEAD_DIM:(j + 1) * HEAD_DIM] = y
        ka_ref[:, 2 * j * HEAD_DIM:(2 * j + 1) * HEAD_DIM] = y.astype(ka_ref.dtype)
        ka_ref[:, (2 * j + 1) * HEAD_DIM:(2 * j + 2) * HEAD_DIM] = onehot
        km_ref[0, :, j * HEAD_DIM:(j + 1) * HEAD_DIM] = jnp.mean(y, axis=0, keepdims=True)


def moba_qknorm(qkv, gq, gk, *, nblk):
    m, n = qkv.shape
    tr = MOBA_BLOCK
    nq = MOBA_Q_HEADS * HEAD_DIM
    nk = MOBA_KV_HEADS * HEAD_DIM
    nbp = nblk + (-nblk % 8)
    assert nbp + 6 <= HEAD_DIM and MOBA_BLOCK <= 256
    return pl.pallas_call(
        functools.partial(_moba_qknorm_kernel, nblk=nblk, nbp=nbp),
        grid=(m // tr,),
        in_specs=[pl.BlockSpec((tr, n), lambda i: (i, 0)),
                  pl.BlockSpec((1, HEAD_DIM), lambda i: (0, 0)),
                  pl.BlockSpec((1, HEAD_DIM), lambda i: (0, 0))],
        out_specs=[pl.BlockSpec((tr, nq), lambda i: (i, 0)),
                   pl.BlockSpec((tr, nk), lambda i: (i, 0)),
                   pl.BlockSpec((tr, 2 * nk), lambda i: (i, 0)),
                   pl.BlockSpec((1, 1, nk), lambda i: (i, 0, 0))],
        out_shape=[jax.ShapeDtypeStruct((m, nq), F32),
                   jax.ShapeDtypeStruct((m, nk), F32),
                   jax.ShapeDtypeStruct((m, 2 * nk), BF16),
                   jax.ShapeDtypeStruct((m // tr, 1, nk), F32)],
        compiler_params=_params(("parallel",)),
        name="moba_qknorm",
    )(qkv, gq.reshape(1, HEAD_DIM), gk.reshape(1, HEAD_DIM))


def _top3(gate, allowed, axis=1):
    nb = gate.shape[axis]
    pos = lax.broadcasted_iota(jnp.int32, gate.shape, axis)
    gm = jnp.where(allowed, gate, -jnp.inf)
    sel = jnp.zeros(gate.shape, F32)
    for _ in range(MOBA_TOPK):
        mx = jnp.max(gm, axis=axis, keepdims=True)
        idx = jnp.min(jnp.where(gm == mx, pos, nb), axis=axis, keepdims=True)
        pick = pos == idx
        sel = jnp.where(pick, 1.0, sel)
        gm = jnp.where(pick, -jnp.inf, gm)
    return jnp.where(allowed, sel, 0.0)


def _online_update(carry, s, ok, v):
    m, l, acc = carry
    s = jnp.where(ok, s, NEG)
    m_new = jnp.maximum(m, jnp.max(s, axis=1, keepdims=True))
    p = jnp.where(ok, jnp.exp(s - m_new), 0.0)
    alpha = jnp.exp(m - m_new)
    l = alpha * l + jnp.sum(p, axis=1, keepdims=True)
    acc = alpha * acc + jnp.dot(p.astype(BF16), v, preferred_element_type=F32)
    return m_new, l, acc


def _moba_prompt_kernel(slopes_ref, q_ref, ka_ref, vt_ref, km_ref, o_ref, *, hstack, nbp):
    kvh = pl.program_id(1)
    qt = pl.program_id(2)
    blk = MOBA_BLOCK
    cols = hstack * blk
    scale = HEAD_DIM ** -0.5
    km = km_ref[0, 0]
    bidx = lax.broadcasted_iota(jnp.int32, (nbp, cols), 0)
    cr = (lax.broadcasted_iota(jnp.int32, (blk, cols), 0)
          - lax.broadcasted_iota(jnp.int32, (blk, cols), 1) % blk)
    qoff = (lax.broadcasted_iota(jnp.int32, (1, cols), 1) % blk).astype(F32)
    arow = lax.broadcasted_iota(jnp.int32, (8, cols), 0)
    off_own = pl.multiple_of(qt * blk, blk)
    ka_own = ka_ref[pl.ds(off_own, blk), :]
    vt_own = vt_ref[0, 0, qt]
    pad = jnp.zeros((HEAD_DIM - nbp - 8, cols), F32)

    def split3(x):
        a = x.astype(BF16).astype(F32)
        b = (x - a).astype(BF16).astype(F32)
        return a, b, x - a - b

    for hg in range(MOBA_GROUP // hstack):
        heads = [hg * hstack + i for i in range(hstack)]
        qs = jnp.concatenate([q_ref[:, h * HEAD_DIM:(h + 1) * HEAD_DIM] for h in heads], axis=0)
        slope = jnp.concatenate([jnp.full((1, blk), slopes_ref[kvh * MOBA_GROUP + h], F32) for h in heads], axis=1)
        gate = lax.dot_general(km, qs, NT, precision=lax.Precision.HIGHEST, preferred_element_type=F32)
        sel = _top3(gate, bidx < qt, axis=0)
        blockbias = jnp.where((sel > 0.5) | (bidx == qt), 0.0, NEG)
        terms = split3(slope) + split3(-slope * qoff)
        alibi = jnp.zeros((8, cols), F32)
        for i, t in enumerate(terms):
            alibi = jnp.where(arow == i, t, alibi)
        qa = jnp.concatenate([(qs * scale).T, blockbias, alibi, pad], axis=0).astype(BF16)
        s = jnp.dot(ka_own, qa, preferred_element_type=F32)
        s = jnp.where(cr <= 0, s, NEG)
        m = jnp.max(s, axis=0, keepdims=True)
        p = jnp.exp(s - m)
        l = jnp.sum(p, axis=0, keepdims=True)
        acc = jnp.dot(vt_own, p.astype(BF16), preferred_element_type=F32)

        def body(n, carry, qa=qa, slope=slope):
            m, l, acc = carry
            off = pl.multiple_of(n * blk, blk)
            s = jnp.dot(ka_ref[pl.ds(off, blk), :], qa, preferred_element_type=F32)
            qterm = slope * (jnp.zeros((1, cols), jnp.int32) + (n - qt) * blk).astype(F32)
            m_new = jnp.maximum(m, jnp.max(s, axis=0, keepdims=True) + qterm)
            p = jnp.exp(s - (m_new - qterm))
            alpha = jnp.exp(m - m_new)
            l = alpha * l + jnp.sum(p, axis=0, keepdims=True)
            acc = alpha * acc + jnp.dot(vt_ref[0, 0, n], p.astype(BF16), preferred_element_type=F32)
            return m_new, l, acc

        m, l, acc = lax.fori_loop(0, qt, body, (m, l, acc))
        o = (acc / l).T
        for i, h in enumerate(heads):
            o_ref[:, h * HEAD_DIM:(h + 1) * HEAD_DIM] = o[i * blk:(i + 1) * blk].astype(o_ref.dtype)


def moba_prompt_attention(qn, ka, vt, kmean, slopes, *, nbatch, seq, hstack=16):
    m = qn.shape[0]
    nqt = seq // MOBA_BLOCK
    nbp = kmean.shape[2]
    assert nqt <= nbp <= HEAD_DIM and nbp % 8 == 0
    gw = MOBA_GROUP * HEAD_DIM
    return pl.pallas_call(
        functools.partial(_moba_prompt_kernel, hstack=hstack, nbp=nbp),
        grid=(nbatch, MOBA_KV_HEADS, nqt),
        in_specs=[pl.BlockSpec(memory_space=pltpu.SMEM),
                  pl.BlockSpec((MOBA_BLOCK, gw), lambda b, k, t: (b * nqt + t, k)),
                  pl.BlockSpec((seq, 2 * HEAD_DIM), lambda b, k, t: (b, k)),
                  pl.BlockSpec((1, 1, nqt, HEAD_DIM, MOBA_BLOCK), lambda b, k, t: (b, k, 0, 0, 0)),
                  pl.BlockSpec((1, 1, nbp, HEAD_DIM), lambda b, k, t: (b, k, 0, 0))],
        out_specs=pl.BlockSpec((MOBA_BLOCK, gw), lambda b, k, t: (b * nqt + t, k)),
        out_shape=jax.ShapeDtypeStruct((m, MOBA_Q_HEADS * HEAD_DIM), BF16),
        compiler_params=_params(("parallel", "parallel", "parallel")),
        name="moba_prompt_attn",
    )(slopes, qn, ka, vt, kmean)


def _kv_chunk(buf, slot, pps):
    pages = [jnp.concatenate([buf[slot, p, pl.ds(j, PAGE_SIZE, stride=MOBA_KV_HEADS), :]
                              for j in range(MOBA_KV_HEADS)], axis=1) for p in range(pps)]
    return jnp.concatenate(pages, axis=0)


def _moba_sample_kernel(pt_ref, ck_hbm, cv_hbm, q_ref, kn_ref, vn_ref, slope_ref, tpos_ref, o_ref,
                        kvbuf, ksem, vsem, s_ref, km_ref, sel_ref, m_ref, l_ref, acc_ref,
                        *, pps, nchunk, npages, past_len):
    c = pl.program_id(1)
    slot = _page_gather(pt_ref, [ck_hbm, cv_hbm], [kvbuf, kvbuf], [ksem, vsem], nsteps=2 * nchunk, npages=npages,
                        pps=pps, chunk_of=lambda st: [(0, st, st < nchunk), (1, st - nchunk, st >= nchunk)])
    scale = HEAD_DIM ** -0.5
    ck = pps * PAGE_SIZE
    bpc = ck // MOBA_BLOCK
    nblk = nchunk * bpc
    slope = slope_ref[...]
    tpos = tpos_ref[...]

    @pl.when(c < nchunk)
    def _scores():
        kc = _kv_chunk(kvbuf, slot, pps)
        s = lax.dot_general(q_ref[0].astype(BF16), kc.astype(BF16), NT, preferred_element_type=F32) * scale
        kpos = (c * ck + lax.broadcasted_iota(jnp.int32, (1, ck), 1)).astype(F32)
        s_ref[c] = s - slope * (tpos - kpos)
        km_ref[c] = jnp.sum(kc.reshape(bpc, MOBA_BLOCK, kc.shape[1]), axis=1) * (1.0 / MOBA_BLOCK)

    @pl.when(c == nchunk)
    def _select():
        km = km_ref[...].reshape(nblk, km_ref.shape[2])
        gate = lax.dot_general(q_ref[0], km, NT, precision=lax.Precision.HIGHEST,
                               preferred_element_type=F32)
        sel_ref[...] = _top3(gate, jnp.full(gate.shape, True))
        m_ref[...] = jnp.full(m_ref.shape, NEG, F32)
        l_ref[...] = jnp.zeros(l_ref.shape, F32)
        acc_ref[...] = jnp.zeros(acc_ref.shape, F32)

    @pl.when(c >= nchunk)
    def _attend():
        cc = c - nchunk
        s = s_ref[cc]
        brow = lax.broadcasted_iota(jnp.int32, (nblk, ck), 0)
        bcol = cc * bpc + lax.broadcasted_iota(jnp.int32, (nblk, ck), 1) // MOBA_BLOCK
        expand = jnp.where(brow == bcol, 1.0, 0.0).astype(BF16)
        ok = jnp.dot(sel_ref[...].astype(BF16), expand, preferred_element_type=F32) > 0.5
        vc = _kv_chunk(kvbuf, slot, pps).astype(BF16)
        m, l, acc = _online_update((m_ref[...], l_ref[...], acc_ref[...]), s, ok, vc)
        m_ref[...] = m
        l_ref[...] = l
        acc_ref[...] = acc

    @pl.when(c == 2 * nchunk - 1)
    def _own_block():
        kn = kn_ref[0]
        s = lax.dot_general(q_ref[0].astype(BF16), kn.astype(BF16), NT, preferred_element_type=F32) * scale
        j = lax.broadcasted_iota(jnp.int32, s.shape, 1).astype(F32)
        dist = tpos - (past_len + j)
        s = s - slope * dist
        m, l, acc = _online_update((m_ref[...], l_ref[...], acc_ref[...]), s, dist >= 0,
                                   vn_ref[0].astype(BF16))
        o_ref[0] = acc / l


def moba_sample_attention(page_table, cache_k, cache_v, q_bd, k_new, v_new, slope_rows, tpos_rows,
                          *, past_len, pps=64):
    ns, r, w = q_bd.shape
    ck = pps * PAGE_SIZE
    assert past_len % MOBA_BLOCK == 0 and ck % MOBA_BLOCK == 0
    npages = past_len // PAGE_SIZE
    nchunk = npages // pps
    nblk = past_len // MOBA_BLOCK
    prows = PAGE_SIZE * MOBA_KV_HEADS
    per_seq = lambda s, c, pt: (s, 0, 0)
    const2 = lambda s, c, pt: (0, 0)
    grid_spec = pltpu.PrefetchScalarGridSpec(
        num_scalar_prefetch=1,
        grid=(ns, 2 * nchunk),
        in_specs=[
            pl.BlockSpec(memory_space=pl.ANY),
            pl.BlockSpec(memory_space=pl.ANY),
            pl.BlockSpec((1, r, w), per_seq),
            pl.BlockSpec((1, 8, w), per_seq),
            pl.BlockSpec((1, 8, w), per_seq),
            pl.BlockSpec((r, 1), const2),
            pl.BlockSpec((r, 1), const2)],
        out_specs=pl.BlockSpec((1, r, w), per_seq),
        scratch_shapes=[pltpu.VMEM((2, pps, prows, HEAD_DIM), F32),
                        pltpu.SemaphoreType.DMA((2,)),
                        pltpu.SemaphoreType.DMA((2,)),
                        pltpu.VMEM((nchunk, r, ck), F32),
                        pltpu.VMEM((nchunk, ck // MOBA_BLOCK, w), F32),
                        pltpu.VMEM((r, nblk), F32),
                        pltpu.VMEM((r, 1), F32),
                        pltpu.VMEM((r, 1), F32),
                        pltpu.VMEM((r, w), F32)])
    return pl.pallas_call(
        functools.partial(_moba_sample_kernel, pps=pps, nchunk=nchunk, npages=npages, past_len=past_len),
        grid_spec=grid_spec,
        out_shape=jax.ShapeDtypeStruct((ns, r, w), F32),
        compiler_params=_params(("arbitrary", "arbitrary")),
        name="moba_sample_attn",
    )(page_table.reshape(-1), cache_k, cache_v, q_bd, k_new, v_new, slope_rows, tpos_rows)


def _mla_kv_prep_kernel(d_ref, gcq_ref, gckv_ref, gk_ref, gks_ref, cos_ref, sin_ref,
                        cq_ref, ckv_ref, ckvb_ref, kpe_ref):
    c0 = MLA_Q_LORA
    c1 = c0 + MLA_KV_LORA
    x = d_ref[:, 0:c0]
    y = x * lax.rsqrt(jnp.mean(x * x, axis=-1, keepdims=True) + NORM_EPS)
    cq_ref[...] = (y * gcq_ref[...]).astype(cq_ref.dtype)
    x = d_ref[:, c0:c1]
    y = (x * lax.rsqrt(jnp.mean(x * x, axis=-1, keepdims=True) + NORM_EPS)) * gckv_ref[...]
    ckv_ref[...] = y
    ckvb_ref[...] = y.astype(ckvb_ref.dtype)
    x = d_ref[:, c1:c1 + 128][:, :MLA_ROPE]
    xs = d_ref[:, c1 + 128:c1 + 256][:, :MLA_ROPE]
    rstd = lax.rsqrt(jnp.mean(x * x, axis=-1, keepdims=True) + NORM_EPS)
    kpe_ref[...] = ((x * rstd) * gk_ref[...]) * cos_ref[...] + ((xs * rstd) * gks_ref[...]) * sin_ref[...]


def mla_kv_prep(d, g_cq, g_ckv, g_kpe, g_kpe_sw, cos2, sin2s, *, tr=256):
    m, n = d.shape
    tab_rows = cos2.shape[0]
    ntab = tab_rows // tr if tab_rows >= tr else 1
    row = lambda i: (i, 0)
    const = lambda i: (0, 0)
    tab = lambda i: (i % ntab, 0)
    return pl.pallas_call(
        _mla_kv_prep_kernel,
        grid=(m // tr,),
        in_specs=[pl.BlockSpec((tr, n), row),
                  pl.BlockSpec((1, MLA_Q_LORA), const),
                  pl.BlockSpec((1, MLA_KV_LORA), const),
                  pl.BlockSpec((1, MLA_ROPE), const),
                  pl.BlockSpec((1, MLA_ROPE), const),
                  pl.BlockSpec((tr, MLA_ROPE), tab),
                  pl.BlockSpec((tr, MLA_ROPE), tab)],
        out_specs=[pl.BlockSpec((tr, MLA_Q_LORA), row),
                   pl.BlockSpec((tr, MLA_KV_LORA), row),
                   pl.BlockSpec((tr, MLA_KV_LORA), row),
                   pl.BlockSpec((tr, MLA_ROPE), row)],
        out_shape=[jax.ShapeDtypeStruct((m, MLA_Q_LORA), BF16),
                   jax.ShapeDtypeStruct((m, MLA_KV_LORA), F32),
                   jax.ShapeDtypeStruct((m, MLA_KV_LORA), BF16),
                   jax.ShapeDtypeStruct((m, MLA_ROPE), F32)],
        compiler_params=_params(("parallel",)),
        name="mla_kv_prep",
    )(d, g_cq.reshape(1, -1), g_ckv.reshape(1, -1), g_kpe.reshape(1, -1), g_kpe_sw.reshape(1, -1), cos2, sin2s)


def _mla_q_prep_kernel(x_ref, gn_ref, gr_ref, grs_ref, cos_ref, sin_ref, qn_ref, qpe_ref):
    scale = MLA_QK_DIM ** -0.5
    nn = MLA_HEADS * MLA_NOPE
    nr = MLA_HEADS * MLA_ROPE
    gn = gn_ref[...]
    gr = gr_ref[...]
    grs = grs_ref[...]
    cos = cos_ref[...]
    sin = sin_ref[...]
    lane = lax.broadcasted_iota(jnp.int32, (x_ref.shape[0], 128), 1)
    first = lane < MLA_ROPE
    for hp in range(MLA_HEADS // 2):
        r = x_ref[:, nn + hp * 128:nn + (hp + 1) * 128]
        rs = x_ref[:, nn + nr + hp * 128:nn + nr + (hp + 1) * 128]
        r2 = r * r
        ss_a = jnp.sum(jnp.where(first, r2, 0.0), axis=-1, keepdims=True)
        ss_b = jnp.sum(jnp.where(first, 0.0, r2), axis=-1, keepdims=True)
        rstd_pair = []
        for sub, ss_r in ((0, ss_a), (1, ss_b)):
            h = 2 * hp + sub
            n = x_ref[:, h * MLA_NOPE:(h + 1) * MLA_NOPE]
            ss = jnp.sum(n * n, axis=-1, keepdims=True) + ss_r
            rstd = lax.rsqrt(ss * (1.0 / MLA_QK_DIM) + NORM_EPS)
            qn_ref[:, h * MLA_NOPE:(h + 1) * MLA_NOPE] = (((n * rstd) * gn) * scale).astype(qn_ref.dtype)
            rstd_pair.append(rstd)
        rstd = jnp.where(first, rstd_pair[0], rstd_pair[1])
        qpe = ((r * rstd) * gr) * cos + ((rs * rstd) * grs) * sin
        qpe_ref[:, hp * 128:(hp + 1) * 128] = (qpe * scale).astype(qpe_ref.dtype)


def mla_q_prep(x, g_n, g_r2, g_rs2, cos4, sin4s, *, tr=256):
    m, n = x.shape
    tab_rows = cos4.shape[0]
    ntab = tab_rows // tr if tab_rows >= tr else 1
    row = lambda i: (i, 0)
    const = lambda i: (0, 0)
    tab = lambda i: (i % ntab, 0)
    return pl.pallas_call(
        _mla_q_prep_kernel,
        grid=(m // tr,),
        in_specs=[pl.BlockSpec((tr, n), row),
                  pl.BlockSpec((1, MLA_NOPE), const),
                  pl.BlockSpec((1, 128), const),
                  pl.BlockSpec((1, 128), const),
                  pl.BlockSpec((tr, 128), tab),
                  pl.BlockSpec((tr, 128), tab)],
        out_specs=[pl.BlockSpec((tr, MLA_HEADS * MLA_NOPE), row),
                   pl.BlockSpec((tr, MLA_HEADS * MLA_ROPE), row)],
        out_shape=[jax.ShapeDtypeStruct((m, MLA_HEADS * MLA_NOPE), BF16),
                   jax.ShapeDtypeStruct((m, MLA_HEADS * MLA_ROPE), BF16)],
        compiler_params=_params(("parallel",)),
        name="mla_q_prep",
    )(x, g_n.reshape(1, -1), g_r2.reshape(1, -1), g_rs2.reshape(1, -1), cos4, sin4s)


def _mla_prompt_kernel(qn_ref, qpe_ref, kn_ref, kpe_ref, v_ref, o_ref, *, tq):
    qt = pl.program_id(2)
    lane = lax.broadcasted_iota(jnp.int32, (tq, 128), 1)
    qpe = qpe_ref[...].astype(F32)
    halves = (lane < MLA_ROPE, lane >= MLA_ROPE)
    qs = [jnp.concatenate([qn_ref[:, i * MLA_NOPE:(i + 1) * MLA_NOPE],
                           jnp.where(halves[i], qpe, 0.0).astype(BF16)], axis=1) for i in range(2)]
    rc = (lax.broadcasted_iota(jnp.int32, (tq, tq), 0) - lax.broadcasted_iota(jnp.int32, (tq, tq), 1))

    def kv_block(n):
        off = pl.multiple_of(n * tq, tq)
        kpe = kpe_ref[pl.ds(off, tq), :]
        ks = [jnp.concatenate([kn_ref[pl.ds(off, tq), i * MLA_NOPE:(i + 1) * MLA_NOPE], kpe], axis=1)
              for i in range(2)]
        vs = [v_ref[pl.ds(off, tq), i * MLA_V_DIM:(i + 1) * MLA_V_DIM] for i in range(2)]
        return ks, vs

    ks, vs = kv_block(qt)
    carry = []
    for i in range(2):
        s = lax.dot_general(qs[i], ks[i], NT, preferred_element_type=F32)
        s = jnp.where(rc >= 0, s, NEG)
        m = jnp.max(s, axis=1, keepdims=True)
        p = jnp.exp(s - m)
        carry += [m, jnp.sum(p, axis=1, keepdims=True), jnp.dot(p.astype(BF16), vs[i], preferred_element_type=F32)]

    def body(n, carry):
        ks, vs = kv_block(n)
        out = []
        for i in range(2):
            m, l, acc = carry[3 * i:3 * i + 3]
            s = lax.dot_general(qs[i], ks[i], NT, preferred_element_type=F32)
            m_new = jnp.maximum(m, jnp.max(s, axis=1, keepdims=True))
            p = jnp.exp(s - m_new)
            alpha = jnp.exp(m - m_new)
            out += [m_new, alpha * l + jnp.sum(p, axis=1, keepdims=True),
                    alpha * acc + jnp.dot(p.astype(BF16), vs[i], preferred_element_type=F32)]
        return tuple(out)

    carry = lax.fori_loop(0, qt, body, tuple(carry))
    for i in range(2):
        m, l, acc = carry[3 * i:3 * i + 3]
        o_ref[:, i * MLA_V_DIM:(i + 1) * MLA_V_DIM] = (acc / l).astype(o_ref.dtype)


def mla_prompt_attention(qn, qpe, kv, kpe, *, nbatch, seq, tq=1024):
    m = qn.shape[0]
    nqt = seq // tq
    npair = MLA_HEADS // 2
    kpe = jnp.concatenate([kpe, kpe], axis=1).astype(BF16)
    return pl.pallas_call(
        functools.partial(_mla_prompt_kernel, tq=tq),
        grid=(nbatch, npair, nqt),
        in_specs=[pl.BlockSpec((tq, 2 * MLA_NOPE), lambda b, h, t: (b * nqt + t, h)),
                  pl.BlockSpec((tq, 2 * MLA_ROPE), lambda b, h, t: (b * nqt + t, h)),
                  pl.BlockSpec((seq, 2 * MLA_NOPE), lambda b, h, t: (b, h)),
                  pl.BlockSpec((seq, 2 * MLA_ROPE), lambda b, h, t: (b, 0)),
                  pl.BlockSpec((seq, 2 * MLA_V_DIM), lambda b, h, t: (b, npair + h))],
        out_specs=pl.BlockSpec((tq, 2 * MLA_V_DIM), lambda b, h, t: (b * nqt + t, h)),
        out_shape=jax.ShapeDtypeStruct((m, MLA_HEADS * MLA_V_DIM), BF16),
        compiler_params=_params(("parallel", "parallel", "parallel")),
        name="mla_prompt_attn",
    )(qn, qpe, kv, kpe, kv)


def _page_gather(pt_ref, srcs, bufs, sems, *, nsteps, npages, pps, chunk_of):
    seq, st = pl.program_id(0), pl.program_id(1)
    lin = seq * nsteps + st
    slot = jnp.bitwise_and(lin, 1)

    def copies(which, page, slot_, j):
        idx = list(range(len(srcs))) if which is None else [which]
        return [pltpu.make_async_copy(srcs[i].at[page], bufs[i].at[slot_, j], sems[i].at[slot_]) for i in idx]

    def issue(seq_, st_, slot_):
        for which, chunk, pred in chunk_of(st_):
            @pl.when(pred)
            def _(which=which, chunk=chunk):
                base = seq_ * npages + chunk * pps

                def body(j, carry):
                    for cp in copies(which, pt_ref[base + j], slot_, j):
                        cp.start()
                    return carry

                lax.fori_loop(0, pps, body, 0)

    @pl.when(lin == 0)
    def _first():
        issue(seq, st, slot)

    wrap = st + 1 == nsteps
    nseq, nst = jnp.where(wrap, seq + 1, seq), jnp.where(wrap, 0, st + 1)

    @pl.when(nseq < pl.num_programs(0))
    def _next():
        issue(nseq, nst, 1 - slot)

    for which, _, pred in chunk_of(st):
        @pl.when(pred)
        def _(which=which):
            def body(j, carry):
                for cp in copies(which, 0, slot, j):
                    cp.wait()
                return carry

            lax.fori_loop(0, pps, body, 0)

    return slot


def _mla_sample_kernel(pt_ref, ckv_hbm, kpe_hbm, ql_ref, qp_ref, cn_ref, rn_ref, tpos_ref, o_ref,
                       cbuf, rbuf, csem, rsem, m_ref, l_ref, acc_ref, *, pps, nchunk, npages, past_len, nsub):
    c = pl.program_id(1)
    slot = _page_gather(pt_ref, [ckv_hbm, kpe_hbm], [cbuf, rbuf], [csem, rsem], nsteps=nchunk, npages=npages,
                        pps=pps, chunk_of=lambda st: [(None, st, st >= 0)])
    ql = ql_ref[0]
    qp = qp_ref[0]

    @pl.when(c == 0)
    def _init():
        m_ref[...] = jnp.full(m_ref.shape, NEG, F32)
        l_ref[...] = jnp.zeros(l_ref.shape, F32)
        acc_ref[...] = jnp.zeros(acc_ref.shape, F32)

    m, l, acc = m_ref[...], l_ref[...], acc_ref[...]
    per = pps // nsub
    ccs, ss = [], []
    for sub in range(nsub):
        cc = cbuf[slot, sub * per:(sub + 1) * per].reshape(per * PAGE_SIZE, MLA_KV_LORA).astype(BF16)
        rr = jnp.concatenate([rbuf[slot, j] for j in range(sub * per, (sub + 1) * per)], axis=1).astype(BF16)
        ccs.append(cc)
        ss.append(lax.dot_general(ql, cc, NT, preferred_element_type=F32)
                  + jnp.dot(qp, rr, preferred_element_type=F32))
    for s, cc in zip(ss, ccs):
        m_new = jnp.maximum(m, jnp.max(s, axis=1, keepdims=True))
        p = jnp.exp(s - m_new)
        alpha = jnp.exp(m - m_new)
        l = alpha * l + jnp.sum(p, axis=1, keepdims=True)
        acc = alpha * acc + jnp.dot(p.astype(BF16), cc, preferred_element_type=F32)
        m = m_new
    m_ref[...] = m
    l_ref[...] = l
    acc_ref[...] = acc

    @pl.when(c == nchunk - 1)
    def _new_rows():
        cn = cn_ref[0].astype(BF16)
        rn = rn_ref[0].astype(BF16)
        s2 = (lax.dot_general(ql, cn, NT, preferred_element_type=F32)
              + lax.dot_general(qp, rn, NT, preferred_element_type=F32))
        j = lax.broadcasted_iota(jnp.int32, s2.shape, 1).astype(F32)
        ok = (past_len + j) <= tpos_ref[...]
        m2, l2, acc2 = _online_update((m_ref[...], l_ref[...], acc_ref[...]), s2, ok, cn)
        o_ref[0] = (acc2 / l2).astype(o_ref.dtype)


def mla_sample_attention(page_table, cache_ckv, cache_kpe, q_lat, q_pe, c_new, r_new, tpos_rows,
                         *, past_len, pps=64, nsub=8):
    ns, r, _ = q_lat.shape
    npages = past_len // PAGE_SIZE
    nchunk = npages // pps
    per_seq = lambda s, c, pt: (s, 0, 0)
    grid_spec = pltpu.PrefetchScalarGridSpec(
        num_scalar_prefetch=1,
        grid=(ns, nchunk),
        in_specs=[
            pl.BlockSpec(memory_space=pl.ANY),
            pl.BlockSpec(memory_space=pl.ANY),
            pl.BlockSpec((1, r, MLA_KV_LORA), per_seq),
            pl.BlockSpec((1, r, MLA_ROPE), per_seq),
            pl.BlockSpec((1, 8, MLA_KV_LORA), per_seq),
            pl.BlockSpec((1, 8, MLA_ROPE), per_seq),
            pl.BlockSpec((r, 1), lambda s, c, pt: (0, 0))],
        out_specs=pl.BlockSpec((1, r, MLA_KV_LORA), per_seq),
        scratch_shapes=[pltpu.VMEM((2, pps, PAGE_SIZE, MLA_KV_LORA), F32),
                        pltpu.VMEM((2, pps, MLA_ROPE, PAGE_SIZE), F32),
                        pltpu.SemaphoreType.DMA((2,)),
                        pltpu.SemaphoreType.DMA((2,)),
                        pltpu.VMEM((r, 1), F32), pltpu.VMEM((r, 1), F32), pltpu.VMEM((r, MLA_KV_LORA), F32)])
    return pl.pallas_call(
        functools.partial(_mla_sample_kernel, pps=pps, nchunk=nchunk, npages=npages, past_len=past_len, nsub=nsub),
        grid_spec=grid_spec,
        out_shape=jax.ShapeDtypeStruct((ns, r, MLA_KV_LORA), BF16),
        compiler_params=_params(("arbitrary", "arbitrary")),
        name="mla_sample_attn",
    )(page_table.reshape(-1), cache_ckv, cache_kpe, q_lat, q_pe, c_new, r_new, tpos_rows)


def _rope_tables(pos, reps):
    half = MLA_ROPE // 2
    inv = ROPE_THETA ** (-jnp.arange(half, dtype=F32) / half)
    ang = pos.astype(F32)[:, None] * inv[None, :]
    cos, sin = jnp.cos(ang), jnp.sin(ang)
    cos2 = jnp.concatenate([cos, cos], axis=1)
    sin2 = jnp.concatenate([-sin, sin], axis=1)
    return jnp.tile(cos2, (1, reps)), jnp.tile(sin2, (1, reps))


def _swap_halves(x):
    half = x.shape[-1] // 2
    return jnp.concatenate([x[..., half:], x[..., :half]], axis=-1)


def _pad_rows8(x, ns):
    t = x.shape[0] // ns
    y = x.reshape(t, ns, x.shape[1]).transpose(1, 0, 2)
    return jnp.pad(y, ((0, 0), (0, 8 - t), (0, 0)))


def _moba_layer(hp, hs, xp, xs, cache_k, cache_v, page_table, w_qkv, g_q, g_k, w_o, *, nbatch, seq, ns, t, past_len):
    nq = MOBA_Q_HEADS * HEAD_DIM
    nk = MOBA_KV_HEADS * HEAD_DIM
    slopes = jnp.exp2(-8.0 * jnp.arange(1, MOBA_Q_HEADS + 1, dtype=F32) / MOBA_Q_HEADS)
    qkv_p = matmul(hp, w_qkv, tm=1024, tn=512, name="moba_qkv_p")
    nqt = seq // MOBA_BLOCK
    qn_p, kn_p, ka_p, km = moba_qknorm(qkv_p, g_q, g_k, nblk=nqt)
    kmean = km.reshape(nbatch, nqt, MOBA_KV_HEADS, HEAD_DIM).transpose(0, 2, 1, 3)
    kmean = jnp.pad(kmean, ((0, 0), (0, 0), (0, -nqt % 8), (0, 0)))
    vt_p = qkv_p[:, nq + nk:].astype(BF16).reshape(nbatch, nqt, MOBA_BLOCK, MOBA_KV_HEADS, HEAD_DIM)
    vt_p = vt_p.transpose(0, 3, 1, 4, 2)
    o_p = moba_prompt_attention(qn_p, ka_p, vt_p, kmean, slopes, nbatch=nbatch, seq=seq)
    xp = matmul(o_p, w_o, xp, tm=1024, tn=512, name="moba_wo_p")
    qkv_s = matmul(hs, w_qkv, tm=hs.shape[0], tn=512, name="moba_qkv_s")
    qn_s, kn_s, _, _ = moba_qknorm(qkv_s, g_q, g_k, nblk=1)
    vn_s = qkv_s[:, nq + nk:]
    r = t * MOBA_GROUP
    q5 = qn_s.reshape(t, ns, MOBA_KV_HEADS, MOBA_GROUP, HEAD_DIM).transpose(1, 2, 0, 3, 4)
    q5 = q5.reshape(ns, MOBA_KV_HEADS, r, HEAD_DIM)
    zeros = jnp.zeros_like(q5[:, 0])
    q_bd = jnp.concatenate([jnp.concatenate([q5[:, 0], zeros], axis=-1),
                            jnp.concatenate([zeros, q5[:, 1]], axis=-1)], axis=1)
    rows = np.arange(MOBA_KV_HEADS * r)
    row_kvh, row_t, row_g = rows // r, (rows % r) // MOBA_GROUP, rows % MOBA_GROUP
    slope_rows = slopes[row_kvh * MOBA_GROUP + row_g].reshape(-1, 1)
    tpos_rows = jnp.asarray((past_len + row_t).astype(np.float32).reshape(-1, 1))
    o_bd = moba_sample_attention(page_table, cache_k, cache_v, q_bd, _pad_rows8(kn_s, ns), _pad_rows8(vn_s, ns),
                                 slope_rows, tpos_rows, past_len=past_len)
    o5 = jnp.stack([o_bd[:, :r, :HEAD_DIM], o_bd[:, r:, HEAD_DIM:]], axis=1)
    o_s = o5.reshape(ns, MOBA_KV_HEADS, t, MOBA_GROUP, HEAD_DIM).transpose(2, 0, 1, 3, 4)
    o_s = o_s.reshape(t * ns, nq).astype(BF16)
    xs = matmul(o_s, w_o, xs, tm=o_s.shape[0], tn=512, name="moba_wo_s")
    return xp, xs, kn_p, qkv_p[:, nq + nk:], kn_s, vn_s


def _mla_layer(hp, hs, xp, xs, cache_ckv, cache_kpe, page_table, w_down, g_cq, w_uq, g_q, g_ckv, g_kpe,
               w_uk, w_uv, w_o, *, nbatch, seq, ns, t, past_len):
    h = MLA_HEADS
    c1 = MLA_Q_LORA + MLA_KV_LORA
    w_kpe = w_down[:, c1:]
    zpad = jnp.zeros((w_down.shape[0], 128 - MLA_ROPE), w_down.dtype)
    w_down_x = jnp.concatenate([w_down[:, :c1], w_kpe, zpad, _swap_halves(w_kpe), zpad], axis=1)
    w_uq3 = w_uq.reshape(MLA_Q_LORA, h, MLA_QK_DIM)
    w_rope = w_uq3[:, :, MLA_NOPE:]
    w_uq_x = jnp.concatenate([w_uq3[:, :, :MLA_NOPE].reshape(MLA_Q_LORA, -1),
                              w_rope.reshape(MLA_Q_LORA, -1),
                              _swap_halves(w_rope).reshape(MLA_Q_LORA, -1)], axis=1)
    w_kv = jnp.concatenate([w_uk.reshape(MLA_KV_LORA, -1), w_uv.reshape(MLA_KV_LORA, -1)], axis=1)
    g_n, g_r = g_q[:MLA_NOPE], g_q[MLA_NOPE:]
    g_r2, g_rs2 = jnp.tile(g_r, 2), jnp.tile(_swap_halves(g_r), 2)
    pos_p = jnp.arange(seq, dtype=jnp.int32)
    pos_s = jnp.repeat(past_len + jnp.arange(t, dtype=jnp.int32), ns)

    def project(hx, pos, tm):
        cos2, sin2 = _rope_tables(pos, 1)
        cos4, sin4 = _rope_tables(pos, 2)
        d = matmul(hx, w_down_x, tm=tm, tn=256, name="mla_down")
        cq, ckv, ckv_b, kpe = mla_kv_prep(d, g_cq, g_ckv, g_kpe, _swap_halves(g_kpe), cos2, sin2)
        qx = matmul(cq, w_uq_x, tm=tm, tn=512, name="mla_uq")
        qn, qpe = mla_q_prep(qx, g_n, g_r2, g_rs2, cos4, sin4)
        return qn, qpe, ckv, ckv_b, kpe

    qn_p, qpe_p, ckv_p, ckvb_p, kpe_p = project(hp, pos_p, 1024)
    kv_p = matmul(ckvb_p, w_kv, tm=1024, tn=512, out_dtype=BF16, name="mla_kv_up")
    o_p = mla_prompt_attention(qn_p, qpe_p, kv_p, kpe_p, nbatch=nbatch, seq=seq)
    xp = matmul(o_p, w_o, xp, tm=1024, tn=512, name="mla_wo_p")
    m_s = hs.shape[0]
    qn_s, qpe_s, ckv_s, _, kpe_s = project(hs, pos_s, m_s)
    w_uk_t = w_uk.transpose(1, 2, 0)
    q_lat = head_matmul(qn_s, w_uk_t, out_dtype=BF16, name="mla_q_lat")
    r = t * h
    q_lat = q_lat.reshape(t, ns, h, MLA_KV_LORA).transpose(1, 0, 2, 3).reshape(ns, r, MLA_KV_LORA)
    q_pe = qpe_s.reshape(t, ns, h, MLA_ROPE).transpose(1, 0, 2, 3).reshape(ns, r, MLA_ROPE)
    tpos_rows = jnp.asarray((past_len + np.arange(r) // h).astype(np.float32).reshape(-1, 1))
    o_lat = mla_sample_attention(page_table, cache_ckv, cache_kpe, q_lat, q_pe,
                                 _pad_rows8(ckv_s, ns), _pad_rows8(kpe_s, ns), tpos_rows, past_len=past_len)
    o_lat = o_lat.reshape(ns, t, h, MLA_KV_LORA).transpose(1, 0, 2, 3).reshape(t * ns, h * MLA_KV_LORA)
    o_s = head_matmul(o_lat, w_uv.transpose(1, 0, 2), out_dtype=BF16, name="mla_o_up")
    xs = matmul(o_s, w_o, xs, tm=m_s, tn=512, name="mla_wo_s")
    return xp, xs, ckv_p, kpe_p, ckv_s, kpe_s


def _conv_ffn(xp, xs, g_norm, state_s, w_gate, w_up, conv_w, conv_b, w_down, *, layer, nbatch, seq, ns):
    hp = rmsnorm_rows(xp, g_norm)
    hs = rmsnorm_rows(xs, g_norm)
    zero_state = jnp.zeros((nbatch, CONV_W - 1, w_gate.shape[2]), F32)
    act_p, st_p = ffn_up_prompt(hp, w_gate, w_up, conv_w, conv_b, zero_state, layer=layer, seq=seq)
    xp = matmul(act_p, w_down, xp, tm=1024, tn=256, name="ffn_down_p", layer=layer, a_buffers=1)
    act_s, st_s = ffn_up_sample(hs, w_gate, w_up, conv_w, conv_b, state_s.transpose(1, 0, 2), layer=layer, nseq=ns)
    xs = matmul(act_s, w_down, xs, tm=xs.shape[0], tn=256, name="ffn_down_s", layer=layer)
    return xp, xs, st_p, st_s.transpose(1, 0, 2)


def kernel(x_prompt, x_sample, cache_moba_k, cache_moba_v, cache_mla_ckv, cache_mla_kpe, state_ffn_conv, page_table, g_mix_norm, g_ffn_norm, moba_w_qkv, moba_g_q, moba_g_k, moba_w_o, mla_w_down, mla_g_cq, mla_w_uq, mla_g_q, mla_g_ckv, mla_g_kpe, mla_w_uk, mla_w_uv, mla_w_o, ffn_w_gate, ffn_w_up, ffn_conv_w, ffn_conv_b, ffn_w_down):
    nbatch, seq, d = x_prompt.shape
    ns, t, _ = x_sample.shape
    n_pool = cache_moba_k.shape[1]
    past_len = page_table.shape[1] * PAGE_SIZE
    depth = g_mix_norm.shape[0]
    dims = dict(nbatch=nbatch, seq=seq, ns=ns, t=t, past_len=past_len)
    xp = x_prompt.reshape(nbatch * seq, d)
    xs = x_sample.transpose(1, 0, 2).reshape(t * ns, d)
    outs = {k: [] for k in ("mk_p", "mv_p", "mk_s", "mv_s", "mc_p", "mr_p", "mc_s", "mr_s", "cv_p", "cv_s")}

    def seq_major(x):
        return x.reshape(t, ns, -1).transpose(1, 0, 2)

    for i in range(depth):
        hp = rmsnorm_rows(xp, g_mix_norm[i])
        hs = rmsnorm_rows(xs, g_mix_norm[i])
        j = i // 2
        if i % 2 == 0:
            ck = cache_moba_k[j].reshape(n_pool, PAGE_SIZE * MOBA_KV_HEADS, HEAD_DIM)
            cv = cache_moba_v[j].reshape(n_pool, PAGE_SIZE * MOBA_KV_HEADS, HEAD_DIM)
            xp, xs, kp, vp, ks, vs = _moba_layer(hp, hs, xp, xs, ck, cv, page_table, moba_w_qkv[j], moba_g_q[j],
                                                 moba_g_k[j], moba_w_o[j], **dims)
            outs["mk_p"].append(kp.reshape(nbatch, seq, MOBA_KV_HEADS, HEAD_DIM))
            outs["mv_p"].append(vp.reshape(nbatch, seq, MOBA_KV_HEADS, HEAD_DIM))
            outs["mk_s"].append(seq_major(ks).reshape(ns, t, MOBA_KV_HEADS, HEAD_DIM))
            outs["mv_s"].append(seq_major(vs).reshape(ns, t, MOBA_KV_HEADS, HEAD_DIM))
        else:
            xp, xs, cp, rp, cs, rs = _mla_layer(hp, hs, xp, xs, cache_mla_ckv[j],
                                                jnp.swapaxes(cache_mla_kpe[j], 1, 2), page_table,
                                                mla_w_down[j], mla_g_cq[j], mla_w_uq[j], mla_g_q[j], mla_g_ckv[j],
                                                mla_g_kpe[j], mla_w_uk[j], mla_w_uv[j], mla_w_o[j], **dims)
            outs["mc_p"].append(cp.reshape(nbatch, seq, -1))
            outs["mr_p"].append(rp.reshape(nbatch, seq, -1))
            outs["mc_s"].append(seq_major(cs))
            outs["mr_s"].append(seq_major(rs))
        xp, xs, st_p, st_s = _conv_ffn(xp, xs, g_ffn_norm[i], state_ffn_conv[i], ffn_w_gate, ffn_w_up,
                                       ffn_conv_w[i], ffn_conv_b[i], ffn_w_down, layer=i, nbatch=nbatch, seq=seq,
                                       ns=ns)
        outs["cv_p"].append(st_p)
        outs["cv_s"].append(st_s)
    y_p = xp.reshape(nbatch, seq, d)
    y_s = seq_major(xs)
    return (y_p, y_s, jnp.stack(outs["mk_p"]), jnp.stack(outs["mv_p"]), jnp.stack(outs["mk_s"]),
            jnp.stack(outs["mv_s"]), jnp.stack(outs["mc_p"]), jnp.stack(outs["mr_p"]), jnp.stack(outs["mc_s"]),
            jnp.stack(outs["mr_s"]), jnp.stack(outs["cv_p"]), jnp.stack(outs["cv_s"]))
```
